```python
import math
import jax, jax.numpy as jnp
from jax import lax
import numpy as np

D_MODEL = 1024
BATCH = 4
SEQ = 8192
DEPTH = 2

N_EVEN = (DEPTH + 1) // 2
N_ODD = DEPTH // 2

LRU_WIDTH = D_MODEL
LRU_HEADS = 8
LRU_BLOCK = LRU_WIDTH // LRU_HEADS
CONV_WIDTH = 4
LRU_C = 8.0
SB_HEADS = 8
SB_HEAD_DIM = 128
SB_WIDTH = SB_HEADS * SB_HEAD_DIM
Q_BLOCK = 128
IN_EVEN = 2 * LRU_WIDTH + 4 * SB_WIDTH
OUT_EVEN = LRU_WIDTH + SB_WIDTH
S5_WIDTH = D_MODEL
S5_GROUP = 16
S5_GROUPS = S5_WIDTH // S5_GROUP
S5_STATE = 64
IN_ODD = 2 * S5_WIDTH

EPS = 1e-6

kernel_name = "hybrid_rglru_stickbreak_s5_trunk"


def rms_norm(x, g):
    x32 = x.astype(jnp.float32)
    y = x32 * lax.rsqrt(jnp.mean(x32 * x32, axis=-1, keepdims=True) + EPS)
    return (y * g.astype(jnp.float32)).astype(x.dtype)


def ada_modulate(x, c, g, w, b):
    mod = jax.nn.silu(c) @ w + b
    shift, scale, gate = jnp.split(mod, 3, axis=-1)
    h = rms_norm(x, g) * (1 + scale[:, None, :]) + shift[:, None, :]
    return h, gate[:, None, :]


def causal_depthwise_conv(x, w, b):
    width = x.shape[-1]
    y = lax.conv_general_dilated(
        x, w[:, None, :].astype(x.dtype), window_strides=(1,),
        padding=[(CONV_WIDTH - 1, 0)], dimension_numbers=("NWC", "WIO", "NWC"),
        feature_group_count=width)
    return y + b


def _linear_combine(e1, e2):
    a1, b1 = e1
    a2, b2 = e2
    return (a2 * a1, a2 * b1 + b2)


def rg_lru(x, wr, br, wi, bi, lam):
    bsz, slen, width = x.shape
    x32 = x.astype(jnp.float32)
    xh = x32.reshape(bsz, slen, LRU_HEADS, LRU_BLOCK)
    r = jax.nn.sigmoid(jnp.einsum("bshi,hij->bshj", xh, wr.astype(jnp.float32)).reshape(bsz, slen, width) + br)
    i = jax.nn.sigmoid(jnp.einsum("bshi,hij->bshj", xh, wi.astype(jnp.float32)).reshape(bsz, slen, width) + bi)
    log_a = LRU_C * r * jax.nn.log_sigmoid(lam.astype(jnp.float32))
    a = jnp.exp(log_a)
    b = jnp.sqrt(-jnp.expm1(2.0 * log_a)) * (i * x32)
    _, h = lax.associative_scan(_linear_combine, (a, b), axis=1)
    return h.astype(x.dtype)


def stick_breaking_attention(q, k, v):
    slen, dh = q.shape[1], q.shape[-1]
    q32 = q.astype(jnp.float32) * (dh ** -0.5)
    k32 = k.astype(jnp.float32)
    v32 = v.astype(jnp.float32)
    outs = []
    for blk in range(slen // Q_BLOCK):
        start = blk * Q_BLOCK
        end = start + Q_BLOCK
        qb = q32[:, start:end]
        kp = k32[:, :end]
        vp = v32[:, :end]
        z = jnp.einsum("bqhd,bkhd->bhqk", qb, kp)
        t_idx = start + jnp.arange(Q_BLOCK)[:, None]
        s_idx = jnp.arange(end)[None, :]
        mask = s_idx < t_idx
        log_keep = jnp.where(mask, jax.nn.log_sigmoid(-z), 0.0)
        later = lax.cumsum(log_keep, axis=3, reverse=True) - log_keep
        w = jnp.where(mask, jnp.exp(jax.nn.log_sigmoid(z) + later), 0.0)
        outs.append(jnp.einsum("bhqk,bkhd->bqhd", w, vp))
    return jnp.concatenate(outs, axis=1).astype(v.dtype)


def s5_ssm(u, lam_re, lam_im, log_dt, b_re, b_im, c_re, c_im, d_skip):
    bsz, slen, width = u.shape
    u32 = u.astype(jnp.float32)
    dt = jnp.exp(log_dt.astype(jnp.float32))[:, None]
    lam_re = lam_re.astype(jnp.float32)
    lam_im = lam_im.astype(jnp.float32)
    decay = jnp.exp(lam_re * dt)
    ang = lam_im * dt
    abar_re = decay * jnp.cos(ang)
    abar_im = decay * jnp.sin(ang)
    den = lam_re * lam_re + lam_im * lam_im
    num_re = abar_re - 1.0
    coef_re = (num_re * lam_re + abar_im * lam_im) / den
    coef_im = (abar_im * lam_re - num_re * lam_im) / den
    b_re = b_re.astype(jnp.float32)
    b_im = b_im.astype(jnp.float32)
    bbar_re = coef_re[..., None] * b_re - coef_im[..., None] * b_im
    bbar_im = coef_re[..., None] * b_im + coef_im[..., None] * b_re
    ug = u32.reshape(bsz, slen, S5_GROUPS, S5_GROUP)
    bu_re = jnp.einsum("bsgc,gpc->bsgp", ug, bbar_re)
    bu_im = jnp.einsum("bsgc,gpc->bsgp", ug, bbar_im)
    a_re = jnp.broadcast_to(abar_re, (1, slen, S5_GROUPS, S5_STATE))
    a_im = jnp.broadcast_to(abar_im, (1, slen, S5_GROUPS, S5_STATE))

    def complex_combine(e1, e2):
        a1r, a1i, b1r, b1i = e1
        a2r, a2i, b2r, b2i = e2
        return (a2r * a1r - a2i * a1i,
                a2r * a1i + a2i * a1r,
                a2r * b1r - a2i * b1i + b2r,
                a2r * b1i + a2i * b1r + b2i)

    _, _, h_re, h_im = lax.associative_scan(complex_combine, (a_re, a_im, bu_re, bu_im), axis=1)
    y = (jnp.einsum("bsgp,gcp->bsgc", h_re, c_re.astype(jnp.float32))
         - jnp.einsum("bsgp,gcp->bsgc", h_im, c_im.astype(jnp.float32)))
    y = y.reshape(bsz, slen, width) + d_skip.astype(jnp.float32) * u32
    return y.astype(u.dtype)


def even_mixer(h, w_in, conv_w, conv_b, wr, br, wi, bi, lam, q_g, k_g, w_out):
    bsz, slen, _ = h.shape
    proj = h @ w_in
    o1 = LRU_WIDTH
    o2 = 2 * LRU_WIDTH
    xa, ga, q, k, v, gb = jnp.split(
        proj, [o1, o2, o2 + SB_WIDTH, o2 + 2 * SB_WIDTH, o2 + 3 * SB_WIDTH], axis=-1)
    ya = rg_lru(causal_depthwise_conv(xa, conv_w, conv_b), wr, br, wi, bi, lam) * jax.nn.silu(ga)
    q = rms_norm(q.reshape(bsz, slen, SB_HEADS, SB_HEAD_DIM), q_g)
    k = rms_norm(k.reshape(bsz, slen, SB_HEADS, SB_HEAD_DIM), k_g)
    v = v.reshape(bsz, slen, SB_HEADS, SB_HEAD_DIM)
    yb = stick_breaking_attention(q, k, v).reshape(bsz, slen, SB_WIDTH) * jax.nn.silu(gb)
    return jnp.concatenate([ya, yb], axis=-1) @ w_out


def odd_mixer(h, w_in, lam_re, lam_im, log_dt, b_re, b_im, c_re, c_im, d_skip, glu_w, glu_b, w_out):
    u, g = jnp.split(h @ w_in, 2, axis=-1)
    y = jax.nn.gelu(s5_ssm(u, lam_re, lam_im, log_dt, b_re, b_im, c_re, c_im, d_skip))
    y = y * jax.nn.sigmoid(y @ glu_w + glu_b)
    return (y * jax.nn.silu(g)) @ w_out


def setup_inputs(seed: int = 0) -> dict:
    key = jax.random.key(seed)
    ks = jax.random.split(key, 32)
    f32 = jnp.float32
    nrm = lambda k, shape, s: jax.random.normal(k, shape, f32) * s
    a0 = jax.random.uniform(ks[12], (N_EVEN, LRU_WIDTH), f32, 0.9, 0.999)
    sig = a0 ** (1.0 / LRU_C)
    lru_lambda = jnp.log(sig) - jnp.log1p(-sig)
    n_idx = jnp.arange(S5_STATE, dtype=f32)
    return {
        "x": nrm(ks[0], (BATCH, SEQ, D_MODEL), 1.0),
        "c": nrm(ks[1], (BATCH, D_MODEL), 1.0),
        "norm_g": 1.0 + nrm(ks[2], (DEPTH, D_MODEL), 0.02),
        "ada_w": nrm(ks[3], (DEPTH, D_MODEL, 3 * D_MODEL), 0.5 * D_MODEL ** -0.5),
        "ada_b": nrm(ks[4], (DEPTH, 3 * D_MODEL), 0.01),
        "w_in_even": nrm(ks[5], (N_EVEN, D_MODEL, IN_EVEN), D_MODEL ** -0.5),
        "conv_w": nrm(ks[6], (N_EVEN, CONV_WIDTH, LRU_WIDTH), CONV_WIDTH ** -0.5),
        "conv_b": nrm(ks[7], (N_EVEN, LRU_WIDTH), 0.01),
        "lru_wr": nrm(ks[8], (N_EVEN, LRU_HEADS, LRU_BLOCK, LRU_BLOCK), LRU_BLOCK ** -0.5),
        "lru_br": nrm(ks[9], (N_EVEN, LRU_WIDTH), 0.01),
        "lru_wi": nrm(ks[10], (N_EVEN, LRU_HEADS, LRU_BLOCK, LRU_BLOCK), LRU_BLOCK ** -0.5),
        "lru_bi": nrm(ks[11], (N_EVEN, LRU_WIDTH), 0.01),
        "lru_lambda": lru_lambda,
        "q_norm_g": 1.0 + nrm(ks[13], (N_EVEN, SB_HEAD_DIM), 0.02),
        "k_norm_g": 1.0 + nrm(ks[14], (N_EVEN, SB_HEAD_DIM), 0.02),
        "w_out_even": nrm(ks[15], (N_EVEN, OUT_EVEN, D_MODEL), OUT_EVEN ** -0.5),
        "w_in_odd": nrm(ks[16], (N_ODD, D_MODEL, IN_ODD), D_MODEL ** -0.5),
        "s5_lambda_re": -0.5 + nrm(ks[17], (N_ODD, S5_GROUPS, S5_STATE), 0.01),
        "s5_lambda_im": math.pi * n_idx + nrm(ks[18], (N_ODD, S5_GROUPS, S5_STATE), 0.01),
        "s5_log_dt": jax.random.uniform(ks[19], (N_ODD, S5_GROUPS), f32, math.log(1e-3), math.log(1e-1)),
        "s5_b_re": nrm(ks[20], (N_ODD, S5_GROUPS, S5_STATE, S5_GROUP), (2 * S5_GROUP) ** -0.5),
        "s5_b_im": nrm(ks[21], (N_ODD, S5_GROUPS, S5_STATE, S5_GROUP), (2 * S5_GROUP) ** -0.5),
        "s5_c_re": nrm(ks[22], (N_ODD, S5_GROUPS, S5_GROUP, S5_STATE), S5_STATE ** -0.5),
        "s5_c_im": nrm(ks[23], (N_ODD, S5_GROUPS, S5_GROUP, S5_STATE), S5_STATE ** -0.5),
        "s5_d": nrm(ks[24], (N_ODD, S5_WIDTH), 1.0),
        "glu_w": nrm(ks[25], (N_ODD, S5_WIDTH, S5_WIDTH), S5_WIDTH ** -0.5),
        "glu_b": nrm(ks[26], (N_ODD, S5_WIDTH), 0.01),
        "w_out_odd": nrm(ks[27], (N_ODD, S5_WIDTH, D_MODEL), S5_WIDTH ** -0.5),
    }


def reference(x, c, norm_g, ada_w, ada_b, w_in_even, conv_w, conv_b, lru_wr, lru_br,
              lru_wi, lru_bi, lru_lambda, q_norm_g, k_norm_g, w_out_even, w_in_odd,
              s5_lambda_re, s5_lambda_im, s5_log_dt, s5_b_re, s5_b_im, s5_c_re, s5_c_im,
              s5_d, glu_w, glu_b, w_out_odd):
    for layer in range(DEPTH):
        h, gate = ada_modulate(x, c, norm_g[layer], ada_w[layer], ada_b[layer])
        j = layer // 2
        if layer % 2 == 0:
            out = even_mixer(h, w_in_even[j], conv_w[j], conv_b[j], lru_wr[j], lru_br[j],
                             lru_wi[j], lru_bi[j], lru_lambda[j], q_norm_g[j], k_norm_g[j],
                             w_out_even[j])
        else:
            out = odd_mixer(h, w_in_odd[j], s5_lambda_re[j], s5_lambda_im[j], s5_log_dt[j],
                            s5_b_re[j], s5_b_im[j], s5_c_re[j], s5_c_im[j], s5_d[j],
                            glu_w[j], glu_b[j], w_out_odd[j])
        x = x + gate * out
    return x
```

```python
import functools
import math

import jax
import jax.numpy as jnp
from jax import lax
from jax.experimental import pallas as pl
from jax.experimental.pallas import tpu as pltpu

F32 = jnp.float32
BF16 = jnp.bfloat16

EPS = 1e-6
LRU_HEADS = 8
LRU_C = 8.0
CONV_WIDTH = 4
SB_HEAD_DIM = 128
S5_GROUP = 16
S5_STATE = 64
S5_CHUNK = 16

SB_DEAD_LOG = -110.0

VMEM_LIMIT = 56 * 1024 * 1024


def _cparams(sem):
    return pltpu.CompilerParams(dimension_semantics=sem, vmem_limit_bytes=VMEM_LIMIT)


def _resident(shape):
    nd = len(shape)
    return pl.BlockSpec(shape, lambda *_: (0,) * nd, pipeline_mode=pl.Buffered(1))


def _silu(x):
    return x * jax.nn.sigmoid(x)


def _dot(a, b):
    return jnp.dot(a, b, preferred_element_type=F32)


def _ada_kernel(c_ref, w_ref, b_ref, o_ref):
    s = _silu(c_ref[...])
    o_ref[0] = jnp.dot(s, w_ref[0], preferred_element_type=F32,
                       precision=lax.Precision.HIGHEST) + b_ref[0]


def _ada(c, ada_w, ada_b):
    depth, d, d3 = ada_w.shape
    bsz = c.shape[0]
    rows = 8
    c_pad = jnp.zeros((rows, d), F32).at[:bsz].set(c)
    tn = 1024
    out = pl.pallas_call(
        _ada_kernel,
        grid=(depth, d3 // tn),
        in_specs=[
            pl.BlockSpec((rows, d), lambda l, n: (0, 0)),
            pl.BlockSpec((1, d, tn), lambda l, n: (l, 0, n)),
            pl.BlockSpec((1, 1, tn), lambda l, n: (l, 0, n)),
        ],
        out_specs=pl.BlockSpec((1, rows, tn), lambda l, n: (l, 0, n)),
        out_shape=jax.ShapeDtypeStruct((depth, rows, d3), F32),
        compiler_params=_cparams(("arbitrary", "arbitrary")),
        name="ada",
    )(c_pad, ada_w, ada_b.reshape(depth, 1, d3))
    return out[:, :bsz]


def _norm_modulate(x, g, mod, d):
    ms = jnp.mean(x * x, axis=-1, keepdims=True)
    y = x * lax.rsqrt(ms + EPS) * g
    return y * (1.0 + mod[:, d:2 * d]) + mod[:, :d]


def _head_rms(t, g_row, scale):
    outs = []
    for h in range(t.shape[1] // SB_HEAD_DIM):
        th = t[:, h * SB_HEAD_DIM:(h + 1) * SB_HEAD_DIM]
        ms = jnp.mean(th * th, axis=-1, keepdims=True)
        outs.append(th * lax.rsqrt(ms + EPS) * (g_row * scale))
    return outs


def _in_even_kernel(x_ref, mod_ref, g_ref, w_ref, qg_ref, kg_ref,
                    xa_ref, ga_ref, q_ref, k_ref, v_ref, gb_ref):
    d = x_ref.shape[-1]
    h = _norm_modulate(x_ref[0], g_ref[...], mod_ref[0], d).astype(BF16)
    xa_ref[0] = _dot(h, w_ref[:, 0 * d:1 * d])
    ga_ref[0] = _dot(h, w_ref[:, 1 * d:2 * d]).astype(BF16)
    q = _dot(h, w_ref[:, 2 * d:3 * d])
    for i, qh in enumerate(_head_rms(q, qg_ref[...], SB_HEAD_DIM ** -0.5)):
        q_ref[0, :, i * SB_HEAD_DIM:(i + 1) * SB_HEAD_DIM] = qh.astype(BF16)
    k = _dot(h, w_ref[:, 3 * d:4 * d])
    for i, kh in enumerate(_head_rms(k, kg_ref[...], 1.0)):
        k_ref[0, :, i * SB_HEAD_DIM:(i + 1) * SB_HEAD_DIM] = kh.astype(BF16)
    v_ref[0] = _dot(h, w_ref[:, 4 * d:5 * d]).astype(BF16)
    gb_ref[0] = _dot(h, w_ref[:, 5 * d:6 * d]).astype(BF16)


def _in_even(x, mod, norm_g, w_in, q_g, k_g, tm):
    bsz, slen, d = x.shape
    tok = pl.BlockSpec((1, tm, d), lambda b, i: (b, i, 0))
    shp = lambda dt: jax.ShapeDtypeStruct((bsz, slen, d), dt)
    return pl.pallas_call(
        _in_even_kernel,
        grid=(bsz, slen // tm),
        in_specs=[
            tok,
            pl.BlockSpec((1, 1, 3 * d), lambda b, i: (b, 0, 0)),
            _resident((1, d)),
            _resident(w_in.shape),
            _resident((1, SB_HEAD_DIM)),
            _resident((1, SB_HEAD_DIM)),
        ],
        out_specs=[tok] * 6,
        out_shape=[shp(F32), shp(BF16), shp(BF16), shp(BF16), shp(BF16), shp(BF16)],
        compiler_params=_cparams(("arbitrary", "arbitrary")),
        name="in_even",
    )(x, mod, norm_g.reshape(1, d), w_in.astype(BF16), q_g.reshape(1, -1), k_g.reshape(1, -1))


def _scan_rows(a, b):
    rows = a.shape[0]
    row = lax.broadcasted_iota(jnp.int32, a.shape, 0)
    d = 1
    while d < rows:
        a_prev = pltpu.roll(a, d, 0)
        b_prev = pltpu.roll(b, d, 0)
        valid = row >= d
        b = jnp.where(valid, a * b_prev + b, b)
        a = jnp.where(valid, a * a_prev, a)
        d *= 2
    return a, b


def _one_minus_exp(y, exp_y):
    series = -y * (1.0 + y * (1 / 2 + y * (1 / 6 + y * (1 / 24 + y * (1 / 120)))))
    return jnp.where(y > -0.1, series, 1.0 - exp_y)


def _lru_kernel(xa_ref, ga_ref, cw_ref, cb_ref, wr_ref, br_ref, wi_ref, bi_ref, lam_ref,
                o_ref, ext_ref, h_ref):
    ts, w = xa_ref.shape[1], xa_ref.shape[2]
    blk = w // LRU_HEADS

    @pl.when(pl.program_id(1) == 0)
    def _():
        ext_ref[0:8, :] = jnp.zeros((8, w), F32)
        h_ref[...] = jnp.zeros_like(h_ref)

    ext_ref[8:, :] = xa_ref[0]
    xc = cb_ref[...] + cw_ref[CONV_WIDTH - 1:CONV_WIDTH, :] * ext_ref[8:, :]
    for kk in range(CONV_WIDTH - 1):
        back = CONV_WIDTH - 1 - kk
        xc = xc + cw_ref[kk:kk + 1, :] * ext_ref[pl.ds(8 - back, ts), :]
    ext_ref[0:8, :] = ext_ref[ts:ts + 8, :]

    xb = xc.astype(BF16)
    rs, is_ = [], []
    for hd in range(LRU_HEADS):
        xh = xb[:, hd * blk:(hd + 1) * blk]
        rs.append(_dot(xh, wr_ref[hd]))
        is_.append(_dot(xh, wi_ref[hd]))
    r = jax.nn.sigmoid(jnp.concatenate(rs, axis=1) + br_ref[...])
    ig = jax.nn.sigmoid(jnp.concatenate(is_, axis=1) + bi_ref[...])
    lam = lam_ref[...]
    log_sig_lam = jnp.minimum(lam, 0.0) - jnp.log1p(jnp.exp(-jnp.abs(lam)))
    log_a = (LRU_C * r) * log_sig_lam
    a = jnp.exp(log_a)
    b = jnp.sqrt(_one_minus_exp(2.0 * log_a, a * a)) * (ig * xc)
    a_cum, b_cum = _scan_rows(a, b)
    h = a_cum * h_ref[7:8, :] + b_cum
    h_ref[...] = h[ts - 8:, :]
    o_ref[0] = (h * _silu(ga_ref[0].astype(F32))).astype(BF16)


def _lru(xa, ga, conv_w, conv_b, wr, br, wi, bi, lam, ts):
    bsz, slen, w = xa.shape
    tok = pl.BlockSpec((1, ts, w), lambda b, i: (b, i, 0))
    row = lambda v: v.reshape(1, w)
    return pl.pallas_call(
        _lru_kernel,
        grid=(bsz, slen // ts),
        in_specs=[tok, tok, _resident(conv_w.shape), _resident((1, w)),
                  _resident(wr.shape), _resident((1, w)), _resident(wi.shape), _resident((1, w)),
                  _resident((1, w))],
        out_specs=tok,
        out_shape=jax.ShapeDtypeStruct((bsz, slen, w), BF16),
        scratch_shapes=[pltpu.VMEM((ts + 8, w), F32), pltpu.VMEM((8, w), F32)],
        compiler_params=_cparams(("arbitrary", "arbitrary")),
        name="lru",
    )(xa, ga, conv_w, row(conv_b), wr.astype(BF16), row(br), wi.astype(BF16), row(bi), row(lam))


def _sb_kernel(q_ref, k_ref, v_ref, gb_ref, o_ref):
    tq, dh = q_ref.shape[1], q_ref.shape[2]
    i = pl.program_id(2)
    q = q_ref[0]
    row = lax.broadcasted_iota(jnp.int32, (tq, tq), 0)
    col = lax.broadcasted_iota(jnp.int32, (tq, tq), 1)
    after = jnp.where(row > col, 1.0, 0.0).astype(BF16)
    causal = col < row

    def block(kb, carry, acc, diag):
        start = pl.multiple_of(kb * tq, tq)
        kblk = k_ref[0, pl.ds(start, tq), :]
        vblk = v_ref[0, pl.ds(start, tq), :]
        z = lax.dot_general(q, kblk, (((1,), (1,)), ((), ())), preferred_element_type=F32)
        sp = jnp.maximum(z, 0.0) + jnp.log1p(jnp.exp(-jnp.abs(z)))
        lk = -sp
        if diag:
            lk = jnp.where(causal, lk, 0.0)
        hi = lk.astype(BF16)
        lo = (lk - hi.astype(F32)).astype(BF16)
        later = _dot(hi, after) + _dot(lo, after)
        wgt = jnp.exp((z - sp) + later + carry)
        if diag:
            wgt = jnp.where(causal, wgt, 0.0)
        acc = acc + _dot(wgt.astype(BF16), vblk)
        carry = carry + jnp.sum(lk, axis=1, keepdims=True)
        return carry, acc

    carry, acc = block(i, jnp.zeros((tq, 1), F32), jnp.zeros((tq, dh), F32), True)

    def cond(st):
        kb, carry, _ = st
        return jnp.logical_and(kb >= 0, jnp.max(carry) > SB_DEAD_LOG)

    def body(st):
        kb, carry, acc = st
        carry, acc = block(kb, carry, acc, False)
        return kb - 1, carry, acc

    _, _, acc = lax.while_loop(cond, body, (i - 1, carry, acc))
    o_ref[0] = (acc * _silu(gb_ref[0].astype(F32))).astype(BF16)


def _sb_attn(q, k, v, gb, tq):
    bsz, slen, w = q.shape
    heads = w // SB_HEAD_DIM
    qspec = pl.BlockSpec((1, tq, SB_HEAD_DIM), lambda b, h, i: (b, i, h))
    kvspec = pl.BlockSpec((1, slen, SB_HEAD_DIM), lambda b, h, i: (b, 0, h))
    return pl.pallas_call(
        _sb_kernel,
        grid=(bsz, heads, slen // tq),
        in_specs=[qspec, kvspec, kvspec, qspec],
        out_specs=qspec,
        out_shape=jax.ShapeDtypeStruct((bsz, slen, w), BF16),
        compiler_params=_cparams(("arbitrary", "arbitrary", "arbitrary")),
        name="sb_attn",
    )(q, k, v, gb)


def _out_even_kernel(x_ref, ya_ref, yb_ref, mod_ref, w_ref, o_ref):
    d = x_ref.shape[-1]
    wa = ya_ref.shape[-1]
    out = _dot(ya_ref[0], w_ref[:wa, :]) + _dot(yb_ref[0], w_ref[wa:, :])
    o_ref[0] = x_ref[0] + mod_ref[0][:, 2 * d:] * out


def _out_even(x, ya, yb, mod, w_out, tm):
    bsz, slen, d = x.shape
    tok = lambda wd: pl.BlockSpec((1, tm, wd), lambda b, i: (b, i, 0))
    return pl.pallas_call(
        _out_even_kernel,
        grid=(bsz, slen // tm),
        in_specs=[tok(d), tok(ya.shape[-1]), tok(yb.shape[-1]),
                  pl.BlockSpec((1, 1, 3 * d), lambda b, i: (b, 0, 0)),
                  _resident(w_out.shape)],
        out_specs=tok(d),
        out_shape=jax.ShapeDtypeStruct((bsz, slen, d), F32),
        compiler_params=_cparams(("arbitrary", "arbitrary")),
        name="out_even",
    )(x, ya, yb, mod, w_out.astype(BF16))


def _in_odd_kernel(x_ref, mod_ref, g_ref, w_ref, u_ref, gate_ref):
    d = x_ref.shape[-1]
    wu = u_ref.shape[-1]
    h = _norm_modulate(x_ref[0], g_ref[...], mod_ref[0], d).astype(BF16)
    u_ref[0] = _dot(h, w_ref[:, :wu]).astype(BF16)
    gate_ref[0] = _dot(h, w_ref[:, wu:]).astype(BF16)


def _in_odd(x, mod, norm_g, w_in, tm):
    bsz, slen, d = x.shape
    wu = w_in.shape[1] // 2
    tok = lambda wd: pl.BlockSpec((1, tm, wd), lambda b, i: (b, i, 0))
    return pl.pallas_call(
        _in_odd_kernel,
        grid=(bsz, slen // tm),
        in_specs=[tok(d), pl.BlockSpec((1, 1, 3 * d), lambda b, i: (b, 0, 0)),
                  _resident((1, d)), _resident(w_in.shape)],
        out_specs=[tok(wu), tok(wu)],
        out_shape=[jax.ShapeDtypeStruct((bsz, slen, wu), BF16)] * 2,
        compiler_params=_cparams(("arbitrary", "arbitrary")),
        name="in_odd",
    )(x, mod, norm_g.reshape(1, d), w_in.astype(BF16))


def _s5_weights(lam_re, lam_im, log_dt, b_re, b_im, c_re, c_im, n_levels):
    hp = lax.Precision.HIGHEST
    L = S5_CHUNK
    dt = jnp.exp(log_dt.astype(F32))[:, None]
    lam_re = lam_re.astype(F32)
    lam_im = lam_im.astype(F32)
    decay = jnp.exp(lam_re * dt)
    ang = lam_im * dt
    abar_re = decay * jnp.cos(ang)
    abar_im = decay * jnp.sin(ang)
    den = lam_re * lam_re + lam_im * lam_im
    num_re = abar_re - 1.0
    coef_re = (num_re * lam_re + abar_im * lam_im) / den
    coef_im = (abar_im * lam_re - num_re * lam_im) / den
    b_re = b_re.astype(F32)
    b_im = b_im.astype(F32)
    bbar_re = coef_re[..., None] * b_re - coef_im[..., None] * b_im
    bbar_im = coef_re[..., None] * b_im + coef_im[..., None] * b_re
    c_re = c_re.astype(F32)
    c_im = c_im.astype(F32)

    def power(n):
        n = n.astype(F32)[:, None, None]
        mag = jnp.exp(n * (lam_re * dt))
        return mag * jnp.cos(n * ang), mag * jnp.sin(n * ang)

    pw_re, pw_im = power(jnp.arange(L + 1))
    ca_re = c_re[None] * pw_re[:, :, None, :] - c_im[None] * pw_im[:, :, None, :]
    ca_im = c_re[None] * pw_im[:, :, None, :] + c_im[None] * pw_re[:, :, None, :]
    kern = (jnp.einsum("ngop,gpi->ngoi", ca_re[:L], bbar_re, precision=hp)
            - jnp.einsum("ngop,gpi->ngoi", ca_im[:L], bbar_im, precision=hp))
    j = jnp.arange(L)
    lag = j[None, :] - j[:, None]
    toep = kern[jnp.clip(lag, 0, L - 1)]
    toep = jnp.where((lag >= 0)[:, :, None, None, None], toep, 0.0)
    toep = toep.transpose(2, 0, 4, 1, 3)
    groups = toep.shape[0]
    toep = toep.reshape(groups, L * S5_GROUP, L * S5_GROUP)

    rev_re, rev_im = pw_re[L - 1 - j], pw_im[L - 1 - j]
    bp_re = rev_re[..., None] * bbar_re[None] - rev_im[..., None] * bbar_im[None]
    bp_im = rev_re[..., None] * bbar_im[None] + rev_im[..., None] * bbar_re[None]
    bpow = jnp.concatenate([bp_re, bp_im], axis=2)
    bpow = bpow.transpose(1, 0, 3, 2).reshape(groups, L * S5_GROUP, 2 * S5_STATE)

    cp = jnp.concatenate([ca_re[1:], -ca_im[1:]], axis=3)
    cpow = cp.transpose(1, 3, 0, 2).reshape(groups, 2 * S5_STATE, L * S5_GROUP)

    st_re, st_im = power(L * (2 ** jnp.arange(n_levels)))
    step = jnp.stack([jnp.concatenate([st_re, st_re], axis=-1),
                      jnp.concatenate([-st_im, st_im], axis=-1)], axis=2)
    step = step.transpose(1, 0, 2, 3)
    return toep, bpow, cpow, step


def _s5_kernel(u_ref, toep_ref, bpow_ref, cpow_ref, step_ref, y_ref):
    u = u_ref[0, 0]
    chunks = u.shape[0]
    x = _dot(u, bpow_ref[0])
    half = x.shape[1] // 2
    row = lax.broadcasted_iota(jnp.int32, x.shape, 0)
    d = 1
    k = 0
    while d < chunks:
        prev = jnp.where(row >= d, pltpu.roll(x, d, 0), 0.0)
        x = x + step_ref[0, k, 0:1, :] * prev + step_ref[0, k, 1:2, :] * pltpu.roll(prev, half, 1)
        d *= 2
        k += 1
    h_in = jnp.where(row >= 1, pltpu.roll(x, 1, 0), 0.0)
    hi = h_in.astype(BF16)
    lo = (h_in - hi.astype(F32)).astype(BF16)
    cp = cpow_ref[0]
    y_ref[0, 0] = _dot(u, toep_ref[0]) + _dot(hi, cp) + _dot(lo, cp)


def _s5(ug, toep, bpow, cpow, step):
    groups, bsz, chunks, width = ug.shape
    gspec = lambda a: pl.BlockSpec((1,) + a.shape[1:], lambda g, b: (g,) + (0,) * (a.ndim - 1))
    tile = pl.BlockSpec((1, 1, chunks, width), lambda g, b: (g, b, 0, 0))
    return pl.pallas_call(
        _s5_kernel,
        grid=(groups, bsz),
        in_specs=[tile, gspec(toep), gspec(bpow), gspec(cpow), gspec(step)],
        out_specs=tile,
        out_shape=jax.ShapeDtypeStruct(ug.shape, F32),
        compiler_params=_cparams(("arbitrary", "arbitrary")),
        name="s5",
    )(ug, toep.astype(BF16), bpow.astype(BF16), cpow.astype(BF16), step)


def _gelu_tanh(x):
    return 0.5 * x * (1.0 + jnp.tanh(math.sqrt(2.0 / math.pi) * (x + 0.044715 * (x * x * x))))


def _out_odd_kernel(x_ref, ys_ref, u_ref, gate_ref, mod_ref, d_ref, gw_ref, gb_ref, w_ref, o_ref):
    d = x_ref.shape[-1]
    y = _gelu_tanh(ys_ref[0] + d_ref[...] * u_ref[0].astype(F32))
    y = y * jax.nn.sigmoid(_dot(y.astype(BF16), gw_ref[...]) + gb_ref[...])
    y = y * _silu(gate_ref[0].astype(F32))
    o_ref[0] = x_ref[0] + mod_ref[0][:, 2 * d:] * _dot(y.astype(BF16), w_ref[...])


def _out_odd(x, ys, u, gate, mod, d_skip, glu_w, glu_b, w_out, tm):
    bsz, slen, d = x.shape
    w = u.shape[-1]
    tok = lambda wd: pl.BlockSpec((1, tm, wd), lambda b, i: (b, i, 0))
    return pl.pallas_call(
        _out_odd_kernel,
        grid=(bsz, slen // tm),
        in_specs=[tok(d), tok(w), tok(w), tok(w),
                  pl.BlockSpec((1, 1, 3 * d), lambda b, i: (b, 0, 0)),
                  _resident((1, w)), _resident(glu_w.shape), _resident((1, w)),
                  _resident(w_out.shape)],
        out_specs=tok(d),
        out_shape=jax.ShapeDtypeStruct((bsz, slen, d), F32),
        compiler_params=_cparams(("arbitrary", "arbitrary")),
        name="out_odd",
    )(x, ys, u, gate, mod, d_skip.reshape(1, w), glu_w.astype(BF16), glu_b.reshape(1, w),
      w_out.astype(BF16))


def _even_layer(x, mod, norm_g, w_in, conv_w, conv_b, wr, br, wi, bi, lam, q_g, k_g, w_out):
    xa, ga, q, k, v, gb = _in_even(x, mod, norm_g, w_in, q_g, k_g, tm=512)
    ya = _lru(xa, ga, conv_w, conv_b, wr, br, wi, bi, lam, ts=256)
    yb = _sb_attn(q, k, v, gb, tq=256)
    return _out_even(x, ya, yb, mod, w_out, tm=512)


def _odd_layer(x, mod, norm_g, w_in, lam_re, lam_im, log_dt, b_re, b_im, c_re, c_im, d_skip,
               glu_w, glu_b, w_out):
    bsz, slen, _ = x.shape
    u, gate = _in_odd(x, mod, norm_g, w_in, tm=512)
    w = u.shape[-1]
    groups = w // S5_GROUP
    chunks = slen // S5_CHUNK
    n_levels = max(1, (chunks - 1).bit_length())
    toep, bpow, cpow, step = _s5_weights(lam_re, lam_im, log_dt, b_re, b_im, c_re, c_im, n_levels)
    ug = u.reshape(bsz, chunks, S5_CHUNK, groups, S5_GROUP).transpose(3, 0, 1, 2, 4)
    ug = ug.reshape(groups, bsz, chunks, S5_CHUNK * S5_GROUP)
    yg = _s5(ug, toep, bpow, cpow, step)
    ys = yg.reshape(groups, bsz, chunks, S5_CHUNK, S5_GROUP).transpose(1, 2, 3, 0, 4)
    ys = ys.reshape(bsz, slen, w)
    return _out_odd(x, ys, u, gate, mod, d_skip, glu_w, glu_b, w_out, tm=512)


def kernel(x, c, norm_g, ada_w, ada_b, w_in_even, conv_w, conv_b, lru_wr, lru_br, lru_wi, lru_bi,
           lru_lambda, q_norm_g, k_norm_g, w_out_even, w_in_odd, s5_lambda_re, s5_lambda_im,
           s5_log_dt, s5_b_re, s5_b_im, s5_c_re, s5_c_im, s5_d, glu_w, glu_b, w_out_odd):
    depth = norm_g.shape[0]
    mods = _ada(c, ada_w, ada_b)
    for layer in range(depth):
        mod = mods[layer][:, None, :]
        j = layer // 2
        if layer % 2 == 0:
            x = _even_layer(x, mod, norm_g[layer], w_in_even[j], conv_w[j], conv_b[j], lru_wr[j],
                            lru_br[j], lru_wi[j], lru_bi[j], lru_lambda[j], q_norm_g[j],
                            k_norm_g[j], w_out_even[j])
        else:
            x = _odd_layer(x, mod, norm_g[layer], w_in_odd[j], s5_lambda_re[j], s5_lambda_im[j],
                           s5_log_dt[j], s5_b_re[j], s5_b_im[j], s5_c_re[j], s5_c_im[j], s5_d[j],
                           glu_w[j], glu_b[j], w_out_odd[j])
    return x
```

```python
import math

import jax
import jax.numpy as jnp
from jax import lax
from jax.experimental import pallas as pl
from jax.experimental.pallas import tpu as pltpu

F32 = jnp.float32
BF16 = jnp.bfloat16

EPS = 1e-6
LANES = 128
LRU_HEADS = 8
LRU_C = 8.0
CONV_WIDTH = 4
SB_HEAD_DIM = 128
S5_GROUP = 16
S5_STATE = 64
S5_CHUNK = 16
S5_ROW = S5_CHUNK * S5_GROUP

SB_DEAD_LOG = -110.0

VMEM_LIMIT = 56 * 1024 * 1024


def _cparams(sem):
    return pltpu.CompilerParams(dimension_semantics=sem, vmem_limit_bytes=VMEM_LIMIT)


def _resident(shape):
    nd = len(shape)
    return pl.BlockSpec(shape, lambda *_: (0,) * nd, pipeline_mode=pl.Buffered(1))


def _silu(x):
    return x * jax.nn.sigmoid(x)


def _dot(a, b):
    return jnp.dot(a, b, preferred_element_type=F32)


def _ada_kernel(c_ref, w_ref, b_ref, o_ref):
    s = _silu(c_ref[...])
    o_ref[0] = jnp.dot(s, w_ref[0], preferred_element_type=F32,
                       precision=lax.Precision.HIGHEST) + b_ref[0]


def _ada(c, ada_w, ada_b):
    depth, d, d3 = ada_w.shape
    bsz = c.shape[0]
    rows = 8
    c_pad = jnp.zeros((rows, d), F32).at[:bsz].set(c)
    tn = 1024
    out = pl.pallas_call(
        _ada_kernel,
        grid=(depth, d3 // tn),
        in_specs=[
            pl.BlockSpec((rows, d), lambda l, n: (0, 0)),
            pl.BlockSpec((1, d, tn), lambda l, n: (l, 0, n)),
            pl.BlockSpec((1, 1, tn), lambda l, n: (l, 0, n)),
        ],
        out_specs=pl.BlockSpec((1, rows, tn), lambda l, n: (l, 0, n)),
        out_shape=jax.ShapeDtypeStruct((depth, rows, d3), F32),
        compiler_params=_cparams(("arbitrary", "arbitrary")),
        name="ada",
    )(c_pad, ada_w, ada_b.reshape(depth, 1, d3))
    return out[:, :bsz]


def _norm_modulate(x, g, mod, d):
    ms = jnp.mean(x * x, axis=-1, keepdims=True)
    y = x * lax.rsqrt(ms + EPS) * g
    return y * (1.0 + mod[:, d:2 * d]) + mod[:, :d]


def _head_rms(t, g_row, scale):
    outs = []
    for h in range(t.shape[1] // SB_HEAD_DIM):
        th = t[:, h * SB_HEAD_DIM:(h + 1) * SB_HEAD_DIM]
        ms = jnp.mean(th * th, axis=-1, keepdims=True)
        outs.append(th * lax.rsqrt(ms + EPS) * (g_row * scale))
    return outs


def _in_even_kernel(x_ref, mod_ref, g_ref, w_ref, qg_ref, kg_ref,
                    xa_ref, ga_ref, q_ref, k_ref, v_ref, gb_ref):
    d = x_ref.shape[-1]
    h = _norm_modulate(x_ref[0], g_ref[...], mod_ref[0], d).astype(BF16)
    xa_ref[0] = _dot(h, w_ref[:, 0 * d:1 * d])
    ga_ref[0] = _dot(h, w_ref[:, 1 * d:2 * d]).astype(BF16)
    q = _dot(h, w_ref[:, 2 * d:3 * d])
    for i, qh in enumerate(_head_rms(q, qg_ref[...], SB_HEAD_DIM ** -0.5)):
        q_ref[0, :, i * SB_HEAD_DIM:(i + 1) * SB_HEAD_DIM] = qh.astype(BF16)
    k = _dot(h, w_ref[:, 3 * d:4 * d])
    for i, kh in enumerate(_head_rms(k, kg_ref[...], 1.0)):
        k_ref[0, :, i * SB_HEAD_DIM:(i + 1) * SB_HEAD_DIM] = kh.astype(BF16)
    v_ref[0] = _dot(h, w_ref[:, 4 * d:5 * d]).astype(BF16)
    gb_ref[0] = _dot(h, w_ref[:, 5 * d:6 * d]).astype(BF16)


def _in_even(x, mod, norm_g, w_in, q_g, k_g, tm):
    bsz, slen, d = x.shape
    tok = pl.BlockSpec((1, tm, d), lambda b, i: (b, i, 0))
    shp = lambda dt: jax.ShapeDtypeStruct((bsz, slen, d), dt)
    return pl.pallas_call(
        _in_even_kernel,
        grid=(bsz, slen // tm),
        in_specs=[
            tok,
            pl.BlockSpec((1, 1, 3 * d), lambda b, i: (b, 0, 0)),
            _resident((1, d)),
            _resident(w_in.shape),
            _resident((1, SB_HEAD_DIM)),
            _resident((1, SB_HEAD_DIM)),
        ],
        out_specs=[tok] * 6,
        out_shape=[shp(F32), shp(BF16), shp(BF16), shp(BF16), shp(BF16), shp(BF16)],
        compiler_params=_cparams(("arbitrary", "arbitrary")),
        name="in_even",
    )(x, mod, norm_g.reshape(1, d), w_in.astype(BF16), q_g.reshape(1, -1), k_g.reshape(1, -1))


def _scan_rows8(a, b):
    sub = lax.broadcasted_iota(jnp.int32, a.shape, 1)
    for d in (1, 2, 4):
        valid = sub >= d
        a_prev = jnp.where(valid, pltpu.roll(a, d, 1), 1.0)
        b_prev = jnp.where(valid, pltpu.roll(b, d, 1), 0.0)
        b = a * b_prev + b
        a = a * a_prev
    return a, b


def _lru_kernel(xa_ref, ga_ref, cw_ref, cb_ref, wr_ref, br_ref, wi_ref, bi_ref, lam_ref,
                o_ref, ext_ref, hbuf_ref, h_ref):
    ts, w = xa_ref.shape[1], xa_ref.shape[2]
    blk = w // LRU_HEADS

    @pl.when(pl.program_id(1) == 0)
    def _():
        ext_ref[0:8, :] = jnp.zeros((8, w), F32)
        h_ref[...] = jnp.zeros_like(h_ref)

    ext_ref[8:, :] = xa_ref[0]
    xc = cb_ref[...] + cw_ref[CONV_WIDTH - 1:CONV_WIDTH, :] * ext_ref[8:, :]
    for kk in range(CONV_WIDTH - 1):
        back = CONV_WIDTH - 1 - kk
        xc = xc + cw_ref[kk:kk + 1, :] * ext_ref[pl.ds(8 - back, ts), :]
    ext_ref[0:8, :] = ext_ref[ts:ts + 8, :]

    xb = xc.astype(BF16)
    rs, is_ = [], []
    for hd in range(LRU_HEADS):
        xh = xb[:, hd * blk:(hd + 1) * blk]
        rs.append(_dot(xh, wr_ref[hd]))
        is_.append(_dot(xh, wi_ref[hd]))
    r = jax.nn.sigmoid(jnp.concatenate(rs, axis=1) + br_ref[...])
    ig = jax.nn.sigmoid(jnp.concatenate(is_, axis=1) + bi_ref[...])
    lam = lam_ref[...]
    log_sig_lam = jnp.minimum(lam, 0.0) - jnp.log(1.0 + jnp.exp(-jnp.abs(lam)))
    log_a = (LRU_C * r) * log_sig_lam
    a = jnp.exp(log_a)
    b = jnp.sqrt(-jnp.tanh(log_a) * (a * a + 1.0)) * (ig * xc)

    groups = ts // 8
    a3, b3 = _scan_rows8(a.reshape(groups, 8, w), b.reshape(groups, 8, w))
    carry = h_ref[7:8, :]
    hg = None
    for gi in range(groups):
        hg = a3[gi] * carry + b3[gi]
        hbuf_ref[8 * gi:8 * gi + 8, :] = hg
        carry = hg[7:8, :]
    h_ref[...] = hg
    o_ref[0] = (hbuf_ref[...] * _silu(ga_ref[0].astype(F32))).astype(BF16)


def _lru(xa, ga, conv_w, conv_b, wr, br, wi, bi, lam, ts):
    bsz, slen, w = xa.shape
    tok = pl.BlockSpec((1, ts, w), lambda b, i: (b, i, 0))
    row = lambda v: v.reshape(1, w)
    return pl.pallas_call(
        _lru_kernel,
        grid=(bsz, slen // ts),
        in_specs=[tok, tok, _resident(conv_w.shape), _resident((1, w)),
                  _resident(wr.shape), _resident((1, w)), _resident(wi.shape), _resident((1, w)),
                  _resident((1, w))],
        out_specs=tok,
        out_shape=jax.ShapeDtypeStruct((bsz, slen, w), BF16),
        scratch_shapes=[pltpu.VMEM((ts + 8, w), F32), pltpu.VMEM((ts, w), F32),
                        pltpu.VMEM((8, w), F32)],
        compiler_params=_cparams(("arbitrary", "arbitrary")),
        name="lru",
    )(xa, ga, conv_w, row(conv_b), wr.astype(BF16), row(br), wi.astype(BF16), row(bi), row(lam))


def _sb_kernel(q_ref, k_ref, v_ref, gb_ref, o_ref):
    tq = q_ref.shape[1]
    dh = SB_HEAD_DIM
    heads = q_ref.shape[2] // dh
    i = pl.program_id(2)
    row = lax.broadcasted_iota(jnp.int32, (tq, tq), 0)
    col = lax.broadcasted_iota(jnp.int32, (tq, tq), 1)
    after = jnp.where(row > col, 1.0, 0.0).astype(BF16)
    after2 = jnp.concatenate([after, after], axis=0)
    causal = col < row

    def block(hd, kb, carry, acc, diag):
        lanes = slice(hd * dh, (hd + 1) * dh)
        start = pl.multiple_of(kb * tq, tq)
        kblk = k_ref[0, pl.ds(start, tq), lanes]
        vblk = v_ref[0, pl.ds(start, tq), lanes]
        z = lax.dot_general(q_ref[0, :, lanes], kblk, (((1,), (1,)), ((), ())),
                            preferred_element_type=F32)
        nl = -jnp.log(1.0 + jnp.exp(-jnp.abs(z)))
        lk = nl - jnp.maximum(z, 0.0)
        if diag:
            lk = jnp.where(causal, lk, 0.0)
        hi = lk.astype(BF16)
        lo = (lk - hi.astype(F32)).astype(BF16)
        later = _dot(jnp.concatenate([hi, lo], axis=1), after2)
        wgt = jnp.exp((nl + jnp.minimum(z, 0.0)) + (later + carry))
        if diag:
            wgt = jnp.where(causal, wgt, 0.0)
        acc = acc + _dot(wgt.astype(BF16), vblk)
        carry = carry + jnp.sum(lk, axis=1, keepdims=True)
        return carry, acc

    carries, accs = [], []
    for hd in range(heads):
        c, a = block(hd, i, jnp.zeros((tq, 1), F32), jnp.zeros((tq, dh), F32), True)
        carries.append(c)
        accs.append(a)

    def alive(carries):
        m = carries[0]
        for c in carries[1:]:
            m = jnp.maximum(m, c)
        return jnp.max(m) > SB_DEAD_LOG

    def cond(st):
        kb, carries, _ = st
        return jnp.logical_and(kb >= 0, alive(carries))

    def body(st):
        kb, carries, accs = st
        new = [block(hd, kb, carries[hd], accs[hd], False) for hd in range(heads)]
        return kb - 1, [n[0] for n in new], [n[1] for n in new]

    _, _, accs = lax.while_loop(cond, body, (i - 1, carries, accs))
    for hd in range(heads):
        lanes = slice(hd * dh, (hd + 1) * dh)
        o_ref[0, :, lanes] = (accs[hd] * _silu(gb_ref[0, :, lanes].astype(F32))).astype(BF16)


def _sb_attn(q, k, v, gb, tq, heads_per_step):
    bsz, slen, w = q.shape
    wd = heads_per_step * SB_HEAD_DIM
    qspec = pl.BlockSpec((1, tq, wd), lambda b, h, i: (b, i, h))
    kvspec = pl.BlockSpec((1, slen, wd), lambda b, h, i: (b, 0, h))
    return pl.pallas_call(
        _sb_kernel,
        grid=(bsz, w // wd, slen // tq),
        in_specs=[qspec, kvspec, kvspec, qspec],
        out_specs=qspec,
        out_shape=jax.ShapeDtypeStruct((bsz, slen, w), BF16),
        compiler_params=_cparams(("arbitrary", "arbitrary", "arbitrary")),
        name="sb_attn",
    )(q, k, v, gb)


def _out_even_kernel(x_ref, ya_ref, yb_ref, mod_ref, w_ref, o_ref):
    d = x_ref.shape[-1]
    wa = ya_ref.shape[-1]
    out = _dot(ya_ref[0], w_ref[:wa, :]) + _dot(yb_ref[0], w_ref[wa:, :])
    o_ref[0] = x_ref[0] + mod_ref[0][:, 2 * d:] * out


def _out_even(x, ya, yb, mod, w_out, tm):
    bsz, slen, d = x.shape
    tok = lambda wd: pl.BlockSpec((1, tm, wd), lambda b, i: (b, i, 0))
    return pl.pallas_call(
        _out_even_kernel,
        grid=(bsz, slen // tm),
        in_specs=[tok(d), tok(ya.shape[-1]), tok(yb.shape[-1]),
                  pl.BlockSpec((1, 1, 3 * d), lambda b, i: (b, 0, 0)),
                  _resident(w_out.shape)],
        out_specs=tok(d),
        out_shape=jax.ShapeDtypeStruct((bsz, slen, d), F32),
        compiler_params=_cparams(("arbitrary", "arbitrary")),
        name="out_even",
    )(x, ya, yb, mod, w_out.astype(BF16))


def _chunk_major_perm(tm):
    rt = tm // S5_CHUNK
    i = jnp.arange(tm)
    src = S5_CHUNK * (i % rt) + i // rt
    return (src[:, None] == jnp.arange(tm)[None, :]).astype(BF16)


def _bands_to_s5_rows(u, ug_ref):
    rt = u.shape[0] // S5_CHUNK
    per_tile = LANES // S5_GROUP
    lane_blk = lax.broadcasted_iota(jnp.int32, (rt, LANES), 1) // S5_GROUP
    for q in range(u.shape[1] // LANES):
        bands = [u[j * rt:(j + 1) * rt, q * LANES:(q + 1) * LANES] for j in range(S5_CHUNK)]
        for gl in range(per_tile):
            for m in range(S5_ROW // LANES):
                out = None
                for jj in range(per_tile):
                    shift = (S5_GROUP * (jj - gl)) % LANES
                    piece = bands[m * per_tile + jj]
                    piece = pltpu.roll(piece, shift, 1) if shift else piece
                    out = piece if out is None else jnp.where(lane_blk == jj, piece, out)
                ug_ref[q * per_tile + gl, 0, :, m * LANES:(m + 1) * LANES] = out.astype(BF16)


def _s5_rows_to_bands(yg_ref, ys_ref):
    rt = ys_ref.shape[0] // S5_CHUNK
    per_tile = LANES // S5_GROUP
    lane_blk = lax.broadcasted_iota(jnp.int32, (rt, LANES), 1) // S5_GROUP
    for j in range(S5_CHUNK):
        m, jj = divmod(j, per_tile)
        for q in range(ys_ref.shape[1] // LANES):
            out = None
            for gl in range(per_tile):
                shift = (S5_GROUP * (gl - jj)) % LANES
                piece = yg_ref[q * per_tile + gl, 0, :, m * LANES:(m + 1) * LANES]
                piece = pltpu.roll(piece, shift, 1) if shift else piece
                out = piece if out is None else jnp.where(lane_blk == gl, piece, out)
            ys_ref[j * rt:(j + 1) * rt, q * LANES:(q + 1) * LANES] = out


def _in_odd_kernel(x_ref, mod_ref, g_ref, perm_ref, w_ref, ug_ref, gate_ref):
    d = x_ref.shape[-1]
    wu = w_ref.shape[1] // 2
    h = _norm_modulate(x_ref[0], g_ref[...], mod_ref[0], d).astype(BF16)
    hp = _dot(perm_ref[...], h).astype(BF16)
    gate_ref[0] = _dot(hp, w_ref[:, wu:]).astype(BF16)
    _bands_to_s5_rows(_dot(hp, w_ref[:, :wu]), ug_ref)


def _in_odd(x, mod, norm_g, w_in, tm):
    bsz, slen, d = x.shape
    wu = w_in.shape[1] // 2
    groups = wu // S5_GROUP
    rt = tm // S5_CHUNK
    tok = lambda wd: pl.BlockSpec((1, tm, wd), lambda b, i: (b, i, 0))
    return pl.pallas_call(
        _in_odd_kernel,
        grid=(bsz, slen // tm),
        in_specs=[tok(d), pl.BlockSpec((1, 1, 3 * d), lambda b, i: (b, 0, 0)),
                  _resident((1, d)), _resident((tm, tm)), _resident(w_in.shape)],
        out_specs=[pl.BlockSpec((groups, 1, rt, S5_ROW), lambda b, i: (0, b, i, 0)), tok(wu)],
        out_shape=[jax.ShapeDtypeStruct((groups, bsz, slen // S5_CHUNK, S5_ROW), BF16),
                   jax.ShapeDtypeStruct((bsz, slen, wu), BF16)],
        compiler_params=_cparams(("arbitrary", "arbitrary")),
        name="in_odd",
    )(x, mod, norm_g.reshape(1, d), _chunk_major_perm(tm), w_in.astype(BF16))


def _s5_weights(lam_re, lam_im, log_dt, b_re, b_im, c_re, c_im, n_levels):
    hp = lax.Precision.HIGHEST
    L = S5_CHUNK
    dt = jnp.exp(log_dt.astype(F32))[:, None]
    lam_re = lam_re.astype(F32)
    lam_im = lam_im.astype(F32)
    decay = jnp.exp(lam_re * dt)
    ang = lam_im * dt
    abar_re = decay * jnp.cos(ang)
    abar_im = decay * jnp.sin(ang)
    den = lam_re * lam_re + lam_im * lam_im
    num_re = abar_re - 1.0
    coef_re = (num_re * lam_re + abar_im * lam_im) / den
    coef_im = (abar_im * lam_re - num_re * lam_im) / den
    b_re = b_re.astype(F32)
    b_im = b_im.astype(F32)
    bbar_re = coef_re[..., None] * b_re - coef_im[..., None] * b_im
    bbar_im = coef_re[..., None] * b_im + coef_im[..., None] * b_re
    c_re = c_re.astype(F32)
    c_im = c_im.astype(F32)

    def power(n):
        n = n.astype(F32)[:, None, None]
        mag = jnp.exp(n * (lam_re * dt))
        return mag * jnp.cos(n * ang), mag * jnp.sin(n * ang)

    pw_re, pw_im = power(jnp.arange(L + 1))
    ca_re = c_re[None] * pw_re[:, :, None, :] - c_im[None] * pw_im[:, :, None, :]
    ca_im = c_re[None] * pw_im[:, :, None, :] + c_im[None] * pw_re[:, :, None, :]
    kern = (jnp.einsum("ngop,gpi->ngoi", ca_re[:L], bbar_re, precision=hp)
            - jnp.einsum("ngop,gpi->ngoi", ca_im[:L], bbar_im, precision=hp))
    j = jnp.arange(L)
    lag = j[None, :] - j[:, None]
    toep = kern[jnp.clip(lag, 0, L - 1)]
    toep = jnp.where((lag >= 0)[:, :, None, None, None], toep, 0.0)
    toep = toep.transpose(2, 0, 4, 1, 3)
    groups = toep.shape[0]
    toep = toep.reshape(groups, S5_ROW, S5_ROW)

    rev_re, rev_im = pw_re[L - 1 - j], pw_im[L - 1 - j]
    bp_re = rev_re[..., None] * bbar_re[None] - rev_im[..., None] * bbar_im[None]
    bp_im = rev_re[..., None] * bbar_im[None] + rev_im[..., None] * bbar_re[None]
    bpow = jnp.concatenate([bp_re, bp_im], axis=2)
    bpow = bpow.transpose(1, 0, 3, 2).reshape(groups, S5_ROW, 2 * S5_STATE)

    cp = jnp.concatenate([ca_re[1:], -ca_im[1:]], axis=3)
    cpow = cp.transpose(1, 3, 0, 2).reshape(groups, 2 * S5_STATE, S5_ROW)

    st_re, st_im = power(L * (2 ** jnp.arange(n_levels)))
    step = jnp.stack([jnp.concatenate([st_re, st_re], axis=-1),
                      jnp.concatenate([-st_im, st_im], axis=-1)], axis=2)
    step = step.transpose(1, 0, 2, 3)
    return toep, bpow, cpow, step


def _s5_kernel(u_ref, toep_ref, bpow_ref, cpow_ref, step_ref, d_ref, y_ref):
    u = u_ref[0, 0]
    chunks = u.shape[0]
    x = _dot(u, bpow_ref[0])
    half = x.shape[1] // 2
    row = lax.broadcasted_iota(jnp.int32, x.shape, 0)
    d = 1
    k = 0
    while d < chunks:
        prev = jnp.where(row >= d, pltpu.roll(x, d, 0), 0.0)
        x = x + step_ref[0, k, 0:1, :] * prev + step_ref[0, k, 1:2, :] * pltpu.roll(prev, half, 1)
        d *= 2
        k += 1
    h_in = jnp.where(row >= 1, pltpu.roll(x, 1, 0), 0.0)
    hi = h_in.astype(BF16)
    lo = (h_in - hi.astype(F32)).astype(BF16)
    cp = cpow_ref[0]
    y = _dot(u, toep_ref[0]) + _dot(hi, cp) + _dot(lo, cp)
    y_ref[0, 0] = y + d_ref[0] * u.astype(F32)


def _s5(ug, toep, bpow, cpow, step, d_rows):
    groups, bsz, chunks, width = ug.shape
    gspec = lambda a: pl.BlockSpec((1,) + a.shape[1:], lambda g, b: (g,) + (0,) * (a.ndim - 1))
    tile = pl.BlockSpec((1, 1, chunks, width), lambda g, b: (g, b, 0, 0))
    return pl.pallas_call(
        _s5_kernel,
        grid=(groups, bsz),
        in_specs=[tile, gspec(toep), gspec(bpow), gspec(cpow), gspec(step), gspec(d_rows)],
        out_specs=tile,
        out_shape=jax.ShapeDtypeStruct(ug.shape, F32),
        compiler_params=_cparams(("arbitrary", "arbitrary")),
        name="s5",
    )(ug, toep.astype(BF16), bpow.astype(BF16), cpow.astype(BF16), step, d_rows)


def _gelu_tanh(x):
    return 0.5 * x * (1.0 + jnp.tanh(math.sqrt(2.0 / math.pi) * (x + 0.044715 * (x * x * x))))


def _out_odd_kernel(x_ref, yg_ref, gate_ref, mod_ref, gw_ref, gb_ref, unperm_ref, w_ref,
                    o_ref, ys_ref):
    d = x_ref.shape[-1]
    _s5_rows_to_bands(yg_ref, ys_ref)
    y = _gelu_tanh(ys_ref[...])
    y = y * jax.nn.sigmoid(_dot(y.astype(BF16), gw_ref[...]) + gb_ref[...])
    y = (y * _silu(gate_ref[0].astype(F32))).astype(BF16)
    y = _dot(unperm_ref[...], y).astype(BF16)
    o_ref[0] = x_ref[0] + mod_ref[0][:, 2 * d:] * _dot(y, w_ref[...])


def _out_odd(x, yg, gate, mod, glu_w, glu_b, w_out, tm):
    bsz, slen, d = x.shape
    groups = yg.shape[0]
    w = gate.shape[-1]
    rt = tm // S5_CHUNK
    tok = lambda wd: pl.BlockSpec((1, tm, wd), lambda b, i: (b, i, 0))
    return pl.pallas_call(
        _out_odd_kernel,
        grid=(bsz, slen // tm),
        in_specs=[tok(d), pl.BlockSpec((groups, 1, rt, S5_ROW), lambda b, i: (0, b, i, 0)), tok(w),
                  pl.BlockSpec((1, 1, 3 * d), lambda b, i: (b, 0, 0)),
                  _resident(glu_w.shape), _resident((1, w)), _resident((tm, tm)),
                  _resident(w_out.shape)],
        out_specs=tok(d),
        out_shape=jax.ShapeDtypeStruct((bsz, slen, d), F32),
        scratch_shapes=[pltpu.VMEM((tm, w), F32)],
        compiler_params=_cparams(("arbitrary", "arbitrary")),
        name="out_odd",
    )(x, yg, gate, mod, glu_w.astype(BF16), glu_b.reshape(1, w), _chunk_major_perm(tm).T,
      w_out.astype(BF16))


def _even_layer(x, mod, norm_g, w_in, conv_w, conv_b, wr, br, wi, bi, lam, q_g, k_g, w_out):
    xa, ga, q, k, v, gb = _in_even(x, mod, norm_g, w_in, q_g, k_g, tm=512)
    ya = _lru(xa, ga, conv_w, conv_b, wr, br, wi, bi, lam, ts=256)
    yb = _sb_attn(q, k, v, gb, tq=256, heads_per_step=4)
    return _out_even(x, ya, yb, mod, w_out, tm=512)


def _odd_layer(x, mod, norm_g, w_in, lam_re, lam_im, log_dt, b_re, b_im, c_re, c_im, d_skip,
               glu_w, glu_b, w_out):
    slen = x.shape[1]
    tm = 512
    ug, gate = _in_odd(x, mod, norm_g, w_in, tm=tm)
    groups = ug.shape[0]
    n_levels = max(1, (slen // S5_CHUNK - 1).bit_length())
    toep, bpow, cpow, step = _s5_weights(lam_re, lam_im, log_dt, b_re, b_im, c_re, c_im, n_levels)
    d_rows = jnp.tile(d_skip.astype(F32).reshape(groups, 1, S5_GROUP), (1, 1, S5_CHUNK))
    yg = _s5(ug, toep, bpow, cpow, step, d_rows)
    return _out_odd(x, yg, gate, mod, glu_w, glu_b, w_out, tm=tm)


def kernel(x, c, norm_g, ada_w, ada_b, w_in_even, conv_w, conv_b, lru_wr, lru_br, lru_wi, lru_bi,
           lru_lambda, q_norm_g, k_norm_g, w_out_even, w_in_odd, s5_lambda_re, s5_lambda_im,
           s5_log_dt, s5_b_re, s5_b_im, s5_c_re, s5_c_im, s5_d, glu_w, glu_b, w_out_odd):
    depth = norm_g.shape[0]
    mods = _ada(c, ada_w, ada_b)
    for layer in range(depth):
        mod = mods[layer][:, None, :]
        j = layer // 2
        if layer % 2 == 0:
            x = _even_layer(x, mod, norm_g[layer], w_in_even[j], conv_w[j], conv_b[j], lru_wr[j],
                            lru_br[j], lru_wi[j], lru_bi[j], lru_lambda[j], q_norm_g[j],
                            k_norm_g[j], w_out_even[j])
        else:
            x = _odd_layer(x, mod, norm_g[layer], w_in_odd[j], s5_lambda_re[j], s5_lambda_im[j],
                           s5_log_dt[j], s5_b_re[j], s5_b_im[j], s5_c_re[j], s5_c_im[j], s5_d[j],
                           glu_w[j], glu_b[j], w_out_odd[j])
    return x
```

```python
import math

import jax
import jax.numpy as jnp
from jax import lax
from jax.experimental import pallas as pl
from jax.experimental.pallas import tpu as pltpu

F32 = jnp.float32
BF16 = jnp.bfloat16

EPS = 1e-6
LANES = 128
LRU_HEADS = 8
LRU_C = 8.0
CONV_WIDTH = 4
SB_HEAD_DIM = 128
S5_GROUP = 16
S5_STATE = 64
S5_CHUNK = 16
S5_ROW = S5_CHUNK * S5_GROUP

LOG2E = math.log2(math.e)
SB_DEAD_LOG2 = 110.0 * LOG2E

VMEM_LIMIT = 56 * 1024 * 1024


def _cparams(sem):
    return pltpu.CompilerParams(dimension_semantics=sem, vmem_limit_bytes=VMEM_LIMIT)


def _resident(shape):
    nd = len(shape)
    return pl.BlockSpec(shape, lambda *_: (0,) * nd, pipeline_mode=pl.Buffered(1))


def _silu(x):
    return x * jax.nn.sigmoid(x)


def _dot(a, b):
    return jnp.dot(a, b, preferred_element_type=F32)


def _ada_kernel(c_ref, w_ref, b_ref, o_ref):
    s = _silu(c_ref[...])
    o_ref[0] = jnp.dot(s, w_ref[0], preferred_element_type=F32,
                       precision=lax.Precision.HIGHEST) + b_ref[0]


def _ada(c, ada_w, ada_b):
    depth, d, d3 = ada_w.shape
    bsz = c.shape[0]
    rows = 8
    c_pad = jnp.zeros((rows, d), F32).at[:bsz].set(c)
    tn = 1024
    out = pl.pallas_call(
        _ada_kernel,
        grid=(depth, d3 // tn),
        in_specs=[
            pl.BlockSpec((rows, d), lambda l, n: (0, 0)),
            pl.BlockSpec((1, d, tn), lambda l, n: (l, 0, n)),
            pl.BlockSpec((1, 1, tn), lambda l, n: (l, 0, n)),
        ],
        out_specs=pl.BlockSpec((1, rows, tn), lambda l, n: (l, 0, n)),
        out_shape=jax.ShapeDtypeStruct((depth, rows, d3), F32),
        compiler_params=_cparams(("arbitrary", "arbitrary")),
        name="ada",
    )(c_pad, ada_w, ada_b.reshape(depth, 1, d3))
    return out[:, :bsz]


def _norm_modulate(x, g, mod, d):
    ms = jnp.mean(x * x, axis=-1, keepdims=True)
    y = x * lax.rsqrt(ms + EPS) * g
    return y * (1.0 + mod[:, d:2 * d]) + mod[:, :d]


def _head_rms(t, g_row, scale):
    outs = []
    for h in range(t.shape[1] // SB_HEAD_DIM):
        th = t[:, h * SB_HEAD_DIM:(h + 1) * SB_HEAD_DIM]
        ms = jnp.mean(th * th, axis=-1, keepdims=True)
        outs.append(th * lax.rsqrt(ms + EPS) * (g_row * scale))
    return outs


def _in_even_kernel(x_ref, mod_ref, g_ref, w_ref, qg_ref, kg_ref,
                    xa_ref, ga_ref, q_ref, k_ref, v_ref, gb_ref):
    d = x_ref.shape[-1]
    h = _norm_modulate(x_ref[0], g_ref[...], mod_ref[0], d).astype(BF16)
    xa_ref[0] = _dot(h, w_ref[:, 0 * d:1 * d])
    ga_ref[0] = _dot(h, w_ref[:, 1 * d:2 * d]).astype(BF16)
    q = _dot(h, w_ref[:, 2 * d:3 * d])
    for i, qh in enumerate(_head_rms(q, qg_ref[...], LOG2E * SB_HEAD_DIM ** -0.5)):
        q_ref[0, :, i * SB_HEAD_DIM:(i + 1) * SB_HEAD_DIM] = qh.astype(BF16)
    k = _dot(h, w_ref[:, 3 * d:4 * d])
    for i, kh in enumerate(_head_rms(k, kg_ref[...], 1.0)):
        k_ref[0, :, i * SB_HEAD_DIM:(i + 1) * SB_HEAD_DIM] = kh.astype(BF16)
    v_ref[0] = _dot(h, w_ref[:, 4 * d:5 * d]).astype(BF16)
    gb_ref[0] = _dot(h, w_ref[:, 5 * d:6 * d]).astype(BF16)


def _in_even(x, mod, norm_g, w_in, q_g, k_g, tm):
    bsz, slen, d = x.shape
    tok = pl.BlockSpec((1, tm, d), lambda b, i: (b, i, 0))
    shp = lambda dt: jax.ShapeDtypeStruct((bsz, slen, d), dt)
    return pl.pallas_call(
        _in_even_kernel,
        grid=(bsz, slen // tm),
        in_specs=[
            tok,
            pl.BlockSpec((1, 1, 3 * d), lambda b, i: (b, 0, 0)),
            _resident((1, d)),
            _resident(w_in.shape),
            _resident((1, SB_HEAD_DIM)),
            _resident((1, SB_HEAD_DIM)),
        ],
        out_specs=[tok] * 6,
        out_shape=[shp(F32), shp(BF16), shp(BF16), shp(BF16), shp(BF16), shp(BF16)],
        compiler_params=_cparams(("arbitrary", "arbitrary")),
        name="in_even",
    )(x, mod, norm_g.reshape(1, d), w_in.astype(BF16), q_g.reshape(1, -1), k_g.reshape(1, -1))


def _scan_rows8(a, b):
    sub = lax.broadcasted_iota(jnp.int32, a.shape, 1)
    for d in (1, 2, 4):
        valid = sub >= d
        a_prev = jnp.where(valid, pltpu.roll(a, d, 1), 1.0)
        b_prev = jnp.where(valid, pltpu.roll(b, d, 1), 0.0)
        b = a * b_prev + b
        a = a * a_prev
    return a, b


def _lru_kernel(xa_ref, ga_ref, cw_ref, cb_ref, wr_ref, br_ref, wi_ref, bi_ref, lam_ref,
                o_ref, tail_ref, hbuf_ref, h_ref):
    ts, w = xa_ref.shape[1], xa_ref.shape[2]
    blk = w // LRU_HEADS

    @pl.when(pl.program_id(1) == 0)
    def _():
        tail_ref[...] = jnp.zeros_like(tail_ref)
        h_ref[...] = jnp.zeros_like(h_ref)

    x = xa_ref[0]
    tail = tail_ref[...]
    sub = lax.broadcasted_iota(jnp.int32, (8, w), 0)
    xc = cb_ref[...] + cw_ref[CONV_WIDTH - 1:CONV_WIDTH, :] * x
    for back in range(1, CONV_WIDTH):
        xs = pltpu.roll(x, back, 0)
        top = jnp.where(sub < back, pltpu.roll(tail, back, 0), xs[:8])
        xs = jnp.concatenate([top, xs[8:]], axis=0)
        xc = xc + cw_ref[CONV_WIDTH - 1 - back:CONV_WIDTH - back, :] * xs
    tail_ref[...] = x[ts - 8:]

    xb = xc.astype(BF16)
    rs, is_ = [], []
    for hd in range(LRU_HEADS):
        xh = xb[:, hd * blk:(hd + 1) * blk]
        rs.append(_dot(xh, wr_ref[hd]))
        is_.append(_dot(xh, wi_ref[hd]))
    r = jax.nn.sigmoid(jnp.concatenate(rs, axis=1) + br_ref[...])
    ig = jax.nn.sigmoid(jnp.concatenate(is_, axis=1) + bi_ref[...])
    lam = lam_ref[...]
    log_sig_lam = jnp.minimum(lam, 0.0) - jnp.log(1.0 + jnp.exp(-jnp.abs(lam)))
    log_a = r * (LRU_C * log_sig_lam)
    a = jnp.exp(log_a)
    v = -jnp.tanh(log_a) * (a * a + 1.0)
    b = (v * lax.rsqrt(jnp.maximum(v, 1e-30))) * (ig * xc)

    groups = ts // 8
    a3, b3 = _scan_rows8(a.reshape(groups, 8, w), b.reshape(groups, 8, w))
    carry = h_ref[7:8, :]
    hg = None
    for gi in range(groups):
        hg = a3[gi] * carry + b3[gi]
        hbuf_ref[8 * gi:8 * gi + 8, :] = hg
        carry = hg[7:8, :]
    h_ref[...] = hg
    o_ref[0] = (hbuf_ref[...] * _silu(ga_ref[0].astype(F32))).astype(BF16)


def _lru(xa, ga, conv_w, conv_b, wr, br, wi, bi, lam, ts):
    bsz, slen, w = xa.shape
    tok = pl.BlockSpec((1, ts, w), lambda b, i: (b, i, 0))
    row = lambda v: v.reshape(1, w)
    return pl.pallas_call(
        _lru_kernel,
        grid=(bsz, slen // ts),
        in_specs=[tok, tok, _resident(conv_w.shape), _resident((1, w)),
                  _resident(wr.shape), _resident((1, w)), _resident(wi.shape), _resident((1, w)),
                  _resident((1, w))],
        out_specs=tok,
        out_shape=jax.ShapeDtypeStruct((bsz, slen, w), BF16),
        scratch_shapes=[pltpu.VMEM((8, w), F32), pltpu.VMEM((ts, w), F32),
                        pltpu.VMEM((8, w), F32)],
        compiler_params=_cparams(("arbitrary", "arbitrary")),
        name="lru",
    )(xa, ga, conv_w, row(conv_b), wr.astype(BF16), row(br), wi.astype(BF16), row(bi), row(lam))


def _sb_kernel(q_ref, k_ref, v_ref, gb_ref, o_ref, acc_ref, carry_ref):
    tq = q_ref.shape[1]
    dh = SB_HEAD_DIM
    heads = q_ref.shape[2] // dh
    row = lax.broadcasted_iota(jnp.int32, (tq, tq), 0)
    col = lax.broadcasted_iota(jnp.int32, (tq, tq), 1)
    after = jnp.where(row > col, 1.0, 0.0).astype(BF16)
    ones = jnp.ones((tq, LANES), BF16)
    causal = col < row

    def scores(hd, start, diag):
        lanes = slice(hd * dh, (hd + 1) * dh)
        kblk = k_ref[0, pl.ds(start, tq), lanes]
        z = lax.dot_general(q_ref[0, :, lanes], kblk, (((1,), (1,)), ((), ())),
                            preferred_element_type=F32)
        mx = jnp.maximum(z, 0.0)
        mn = jnp.minimum(z, 0.0)
        l = jnp.log2(1.0 + jnp.exp2(mn - mx))
        sp = l + mx
        if diag:
            sp = jnp.where(causal, sp, 0.0)
        sp = sp.astype(BF16)
        rowsum = _dot(sp, ones)
        old = None if diag else carry_ref[hd]
        carry_ref[hd] = rowsum if diag else old + rowsum
        return sp, mn - l, old

    def weights(hd, start, diag, sp, log_beta, old):
        lanes = slice(hd * dh, (hd + 1) * dh)
        vblk = v_ref[0, pl.ds(start, tq), lanes]
        later = _dot(sp, after)
        if diag:
            wgt = jnp.where(causal, jnp.exp2(log_beta - later), 0.0)
            acc_ref[hd] = _dot(wgt.astype(BF16), vblk)
        else:
            later = later + jnp.concatenate([old] * (tq // LANES), axis=1)
            acc_ref[hd] += _dot(jnp.exp2(log_beta - later).astype(BF16), vblk)

    def sweep(start, diag):
        st = [scores(hd, start, diag) for hd in range(heads)]
        m = carry_ref[0]
        for hd in range(1, heads):
            m = jnp.minimum(m, carry_ref[hd])
        alive = (jnp.min(m) < SB_DEAD_LOG2).astype(jnp.int32)
        for hd in range(heads):
            weights(hd, start, diag, *st[hd])
        return alive

    q0 = pl.multiple_of(pl.program_id(2) * tq, tq)

    def cond(st):
        end, alive = st
        return jnp.logical_and(end > 0, alive > 0)

    def body(st):
        start = pl.multiple_of(st[0] - tq, tq)
        return start, sweep(start, False)

    lax.while_loop(cond, body, (q0, sweep(q0, True)))
    for hd in range(heads):
        lanes = slice(hd * dh, (hd + 1) * dh)
        o_ref[0, :, lanes] = (acc_ref[hd] * _silu(gb_ref[0, :, lanes].astype(F32))).astype(BF16)


def _sb_attn(q, k, v, gb, tq, heads_per_step):
    bsz, slen, w = q.shape
    wd = heads_per_step * SB_HEAD_DIM
    qspec = pl.BlockSpec((1, tq, wd), lambda b, h, i: (b, i, h))
    kvspec = pl.BlockSpec((1, slen, wd), lambda b, h, i: (b, 0, h), pipeline_mode=pl.Buffered(1))
    return pl.pallas_call(
        _sb_kernel,
        grid=(bsz, w // wd, slen // tq),
        in_specs=[qspec, kvspec, kvspec, qspec],
        out_specs=qspec,
        out_shape=jax.ShapeDtypeStruct((bsz, slen, w), BF16),
        scratch_shapes=[pltpu.VMEM((heads_per_step, tq, SB_HEAD_DIM), F32),
                        pltpu.VMEM((heads_per_step, tq, LANES), F32)],
        compiler_params=_cparams(("arbitrary", "arbitrary", "arbitrary")),
        name="sb_attn",
    )(q, k, v, gb)


def _out_even_kernel(x_ref, ya_ref, yb_ref, mod_ref, w_ref, o_ref):
    d = x_ref.shape[-1]
    wa = ya_ref.shape[-1]
    out = _dot(ya_ref[0], w_ref[:wa, :]) + _dot(yb_ref[0], w_ref[wa:, :])
    o_ref[0] = x_ref[0] + mod_ref[0][:, 2 * d:] * out


def _out_even(x, ya, yb, mod, w_out, tm):
    bsz, slen, d = x.shape
    tok = lambda wd: pl.BlockSpec((1, tm, wd), lambda b, i: (b, i, 0))
    return pl.pallas_call(
        _out_even_kernel,
        grid=(bsz, slen // tm),
        in_specs=[tok(d), tok(ya.shape[-1]), tok(yb.shape[-1]),
                  pl.BlockSpec((1, 1, 3 * d), lambda b, i: (b, 0, 0)),
                  _resident(w_out.shape)],
        out_specs=tok(d),
        out_shape=jax.ShapeDtypeStruct((bsz, slen, d), F32),
        compiler_params=_cparams(("arbitrary", "arbitrary")),
        name="out_even",
    )(x, ya, yb, mod, w_out.astype(BF16))


def _chunk_major_perm(tm):
    rt = tm // S5_CHUNK
    i = jnp.arange(tm)
    src = S5_CHUNK * (i % rt) + i // rt
    return (src[:, None] == jnp.arange(tm)[None, :]).astype(BF16)


def _lane_block_transpose(src, put):
    per_tile = LANES // S5_GROUP
    lane_blk = lax.broadcasted_iota(jnp.int32, src[0].shape, 1) // S5_GROUP
    rolled = []
    for d in range(per_tile):
        t = src[d]
        for a in range(1, per_tile):
            t = jnp.where(lane_blk == a, src[(a + d) % per_tile], t)
        rolled.append(pltpu.roll(t, S5_GROUP * d, 1) if d else t)
    for a in range(per_tile):
        out = rolled[(-a) % per_tile]
        for b in range(1, per_tile):
            out = jnp.where(lane_blk == b, rolled[(b - a) % per_tile], out)
        put(a, out)


def _bands_to_s5_rows(u, ug_ref):
    rt = u.shape[0] // S5_CHUNK
    per_tile = LANES // S5_GROUP
    for q in range(u.shape[1] // LANES):
        for m in range(S5_ROW // LANES):
            bands = [u[(m * per_tile + jj) * rt:(m * per_tile + jj + 1) * rt,
                       q * LANES:(q + 1) * LANES] for jj in range(per_tile)]

            def put(gl, rows, q=q, m=m):
                ug_ref[q * per_tile + gl, 0, :, m * LANES:(m + 1) * LANES] = rows.astype(BF16)

            _lane_block_transpose(bands, put)


def _s5_rows_to_bands(yg_ref, ys_ref):
    rt = ys_ref.shape[0] // S5_CHUNK
    per_tile = LANES // S5_GROUP
    for q in range(ys_ref.shape[1] // LANES):
        for m in range(S5_ROW // LANES):
            rows = [yg_ref[q * per_tile + gl, 0, :, m * LANES:(m + 1) * LANES]
                    for gl in range(per_tile)]

            def put(jj, band, q=q, m=m):
                j = m * per_tile + jj
                ys_ref[j * rt:(j + 1) * rt, q * LANES:(q + 1) * LANES] = band

            _lane_block_transpose(rows, put)


def _in_odd_kernel(x_ref, mod_ref, g_ref, perm_ref, w_ref, ug_ref, gate_ref):
    d = x_ref.shape[-1]
    wu = w_ref.shape[1] // 2
    h = _norm_modulate(x_ref[0], g_ref[...], mod_ref[0], d).astype(BF16)
    hp = _dot(perm_ref[...], h).astype(BF16)
    gate_ref[0] = _dot(hp, w_ref[:, wu:]).astype(BF16)
    _bands_to_s5_rows(_dot(hp, w_ref[:, :wu]), ug_ref)


def _in_odd(x, mod, norm_g, w_in, tm):
    bsz, slen, d = x.shape
    wu = w_in.shape[1] // 2
    groups = wu // S5_GROUP
    rt = tm // S5_CHUNK
    tok = lambda wd: pl.BlockSpec((1, tm, wd), lambda b, i: (b, i, 0))
    return pl.pallas_call(
        _in_odd_kernel,
        grid=(bsz, slen // tm),
        in_specs=[tok(d), pl.BlockSpec((1, 1, 3 * d), lambda b, i: (b, 0, 0)),
                  _resident((1, d)), _resident((tm, tm)), _resident(w_in.shape)],
        out_specs=[pl.BlockSpec((groups, 1, rt, S5_ROW), lambda b, i: (0, b, i, 0)), tok(wu)],
        out_shape=[jax.ShapeDtypeStruct((groups, bsz, slen // S5_CHUNK, S5_ROW), BF16),
                   jax.ShapeDtypeStruct((bsz, slen, wu), BF16)],
        compiler_params=_cparams(("arbitrary", "arbitrary")),
        name="in_odd",
    )(x, mod, norm_g.reshape(1, d), _chunk_major_perm(tm), w_in.astype(BF16))


def _s5_weights(lam_re, lam_im, log_dt, b_re, b_im, c_re, c_im, n_levels):
    hp = lax.Precision.HIGHEST
    L = S5_CHUNK
    groups = lam_re.shape[0]
    dt = jnp.exp(log_dt.astype(F32))[:, None]
    lam_re = lam_re.astype(F32)
    lam_im = lam_im.astype(F32)
    decay = jnp.exp(lam_re * dt)
    ang = lam_im * dt
    abar_re = decay * jnp.cos(ang)
    abar_im = decay * jnp.sin(ang)
    den = lam_re * lam_re + lam_im * lam_im
    num_re = abar_re - 1.0
    coef_re = (num_re * lam_re + abar_im * lam_im) / den
    coef_im = (abar_im * lam_re - num_re * lam_im) / den
    b_re = b_re.astype(F32)
    b_im = b_im.astype(F32)
    bbar_re = coef_re[..., None] * b_re - coef_im[..., None] * b_im
    bbar_im = coef_re[..., None] * b_im + coef_im[..., None] * b_re
    c_re = c_re.astype(F32)
    c_im = c_im.astype(F32)

    def power(n):
        n = n.astype(F32)[None, :, None]
        mag = jnp.exp(n * (lam_re * dt)[:, None, :])
        arg = n * ang[:, None, :]
        return mag * jnp.cos(arg), mag * jnp.sin(arg)

    pw_re, pw_im = power(jnp.arange(L + 1))
    ca_re = c_re[:, None] * pw_re[:, :L, None, :] - c_im[:, None] * pw_im[:, :L, None, :]
    ca_im = c_re[:, None] * pw_im[:, :L, None, :] + c_im[:, None] * pw_re[:, :L, None, :]
    strip = (jnp.einsum("gnop,gpi->gino", ca_re, bbar_re, precision=hp)
             - jnp.einsum("gnop,gpi->gino", ca_im, bbar_im, precision=hp)).reshape(groups, S5_GROUP, S5_ROW)
    padded = jnp.concatenate([jnp.zeros_like(strip), strip], axis=-1)
    toep = jnp.stack([padded[:, :, S5_ROW - S5_GROUP * ji:2 * S5_ROW - S5_GROUP * ji]
                      for ji in range(L)], axis=1).reshape(groups, S5_ROW, S5_ROW)

    rev_re = pw_re[:, L - 1::-1][:, :L, None, :]
    rev_im = pw_im[:, L - 1::-1][:, :L, None, :]
    bt_re = bbar_re.transpose(0, 2, 1)[:, None]
    bt_im = bbar_im.transpose(0, 2, 1)[:, None]
    bpow = jnp.concatenate([rev_re * bt_re - rev_im * bt_im, rev_re * bt_im + rev_im * bt_re],
                           axis=-1).reshape(groups, S5_ROW, 2 * S5_STATE)

    ct_re = c_re.transpose(0, 2, 1)[:, :, None, :]
    ct_im = c_im.transpose(0, 2, 1)[:, :, None, :]
    pt_re = pw_re[:, 1:].transpose(0, 2, 1)[..., None]
    pt_im = pw_im[:, 1:].transpose(0, 2, 1)[..., None]
    cpow = jnp.concatenate([ct_re * pt_re - ct_im * pt_im, -(ct_re * pt_im + ct_im * pt_re)],
                           axis=1).reshape(groups, 2 * S5_STATE, S5_ROW)

    st_re, st_im = power(L * (2 ** jnp.arange(n_levels)))
    step = jnp.stack([jnp.concatenate([st_re, st_re], axis=-1),
                      jnp.concatenate([-st_im, st_im], axis=-1)], axis=2)
    return toep, bpow, cpow, step


def _s5_kernel(u_ref, toep_ref, bpow_ref, cpow_ref, step_ref, d_ref, y_ref):
    chunks = u_ref.shape[2]
    half = bpow_ref.shape[2] // 2
    row = lax.broadcasted_iota(jnp.int32, (chunks, 2 * half), 0)
    cp = cpow_ref[0]
    for b in range(u_ref.shape[1]):
        u = u_ref[0, b]
        x = _dot(u, bpow_ref[0])
        d = 1
        k = 0
        while d < chunks:
            prev = jnp.where(row >= d, pltpu.roll(x, d, 0), 0.0)
            x = (x + step_ref[0, k, 0:1, :] * prev
                 + step_ref[0, k, 1:2, :] * pltpu.roll(prev, half, 1))
            d *= 2
            k += 1
        h_in = jnp.where(row >= 1, pltpu.roll(x, 1, 0), 0.0)
        hi = h_in.astype(BF16)
        lo = (h_in - hi.astype(F32)).astype(BF16)
        y = _dot(u, toep_ref[0]) + _dot(hi, cp) + _dot(lo, cp)
        y_ref[0, b] = y + d_ref[0] * u.astype(F32)


def _s5(ug, toep, bpow, cpow, step, d_rows):
    groups, bsz, chunks, width = ug.shape
    gspec = lambda a: pl.BlockSpec((1,) + a.shape[1:], lambda g: (g,) + (0,) * (a.ndim - 1))
    tile = pl.BlockSpec((1, bsz, chunks, width), lambda g: (g, 0, 0, 0))
    return pl.pallas_call(
        _s5_kernel,
        grid=(groups,),
        in_specs=[tile, gspec(toep), gspec(bpow), gspec(cpow), gspec(step), gspec(d_rows)],
        out_specs=tile,
        out_shape=jax.ShapeDtypeStruct(ug.shape, F32),
        compiler_params=_cparams(("arbitrary",)),
        name="s5",
    )(ug, toep.astype(BF16), bpow.astype(BF16), cpow.astype(BF16), step, d_rows)


def _gelu_tanh(x):
    return 0.5 * x * (1.0 + jnp.tanh(math.sqrt(2.0 / math.pi) * (x + 0.044715 * (x * x * x))))


def _out_odd_kernel(x_ref, yg_ref, gate_ref, mod_ref, gw_ref, gb_ref, unperm_ref, w_ref,
                    o_ref, ys_ref):
    d = x_ref.shape[-1]
    _s5_rows_to_bands(yg_ref, ys_ref)
    y = _gelu_tanh(ys_ref[...])
    y = y * jax.nn.sigmoid(_dot(y.astype(BF16), gw_ref[...]) + gb_ref[...])
    y = (y * _silu(gate_ref[0].astype(F32))).astype(BF16)
    y = _dot(unperm_ref[...], y).astype(BF16)
    o_ref[0] = x_ref[0] + mod_ref[0][:, 2 * d:] * _dot(y, w_ref[...])


def _out_odd(x, yg, gate, mod, glu_w, glu_b, w_out, tm):
    bsz, slen, d = x.shape
    groups = yg.shape[0]
    w = gate.shape[-1]
    rt = tm // S5_CHUNK
    tok = lambda wd: pl.BlockSpec((1, tm, wd), lambda b, i: (b, i, 0))
    return pl.pallas_call(
        _out_odd_kernel,
        grid=(bsz, slen // tm),
        in_specs=[tok(d), pl.BlockSpec((groups, 1, rt, S5_ROW), lambda b, i: (0, b, i, 0)), tok(w),
                  pl.BlockSpec((1, 1, 3 * d), lambda b, i: (b, 0, 0)),
                  _resident(glu_w.shape), _resident((1, w)), _resident((tm, tm)),
                  _resident(w_out.shape)],
        out_specs=tok(d),
        out_shape=jax.ShapeDtypeStruct((bsz, slen, d), F32),
        scratch_shapes=[pltpu.VMEM((tm, w), F32)],
        compiler_params=_cparams(("arbitrary", "arbitrary")),
        name="out_odd",
    )(x, yg, gate, mod, glu_w.astype(BF16), glu_b.reshape(1, w), _chunk_major_perm(tm).T,
      w_out.astype(BF16))


def _even_layer(x, mod, norm_g, w_in, conv_w, conv_b, wr, br, wi, bi, lam, q_g, k_g, w_out):
    xa, ga, q, k, v, gb = _in_even(x, mod, norm_g, w_in, q_g, k_g, tm=512)
    ya = _lru(xa, ga, conv_w, conv_b, wr, br, wi, bi, lam, ts=256)
    yb = _sb_attn(q, k, v, gb, tq=256, heads_per_step=8)
    return _out_even(x, ya, yb, mod, w_out, tm=512)


def _odd_layer(x, mod, norm_g, w_in, lam_re, lam_im, log_dt, b_re, b_im, c_re, c_im, d_skip,
               glu_w, glu_b, w_out):
    slen = x.shape[1]
    tm = 512
    ug, gate = _in_odd(x, mod, norm_g, w_in, tm=tm)
    groups = ug.shape[0]
    n_levels = max(1, (slen // S5_CHUNK - 1).bit_length())
    toep, bpow, cpow, step = _s5_weights(lam_re, lam_im, log_dt, b_re, b_im, c_re, c_im, n_levels)
    d_rows = jnp.tile(d_skip.astype(F32).reshape(groups, 1, S5_GROUP), (1, 1, S5_CHUNK))
    yg = _s5(ug, toep, bpow, cpow, step, d_rows)
    return _out_odd(x, yg, gate, mod, glu_w, glu_b, w_out, tm=tm)


def kernel(x, c, norm_g, ada_w, ada_b, w_in_even, conv_w, conv_b, lru_wr, lru_br, lru_wi, lru_bi,
           lru_lambda, q_norm_g, k_norm_g, w_out_even, w_in_odd, s5_lambda_re, s5_lambda_im,
           s5_log_dt, s5_b_re, s5_b_im, s5_c_re, s5_c_im, s5_d, glu_w, glu_b, w_out_odd):
    depth = norm_g.shape[0]
    mods = _ada(c, ada_w, ada_b)
    for layer in range(depth):
        mod = mods[layer][:, None, :]
        j = layer // 2
        if layer % 2 == 0:
            x = _even_layer(x, mod, norm_g[layer], w_in_even[j], conv_w[j], conv_b[j], lru_wr[j],
                            lru_br[j], lru_wi[j], lru_bi[j], lru_lambda[j], q_norm_g[j],
                            k_norm_g[j], w_out_even[j])
        else:
            x = _odd_layer(x, mod, norm_g[layer], w_in_odd[j], s5_lambda_re[j], s5_lambda_im[j],
                           s5_log_dt[j], s5_b_re[j], s5_b_im[j], s5_c_re[j], s5_c_im[j], s5_d[j],
                           glu_w[j], glu_b[j], w_out_odd[j])
    return x
```

```python
import math

import jax
import jax.numpy as jnp
from jax import lax
from jax.experimental import pallas as pl
from jax.experimental.pallas import tpu as pltpu

F32 = jnp.float32
BF16 = jnp.bfloat16

EPS = 1e-6
LANES = 128
LRU_HEADS = 8
LRU_C = 8.0
CONV_WIDTH = 4
SB_HEAD_DIM = 128
S5_GROUP = 16
S5_STATE = 64
S5_CHUNK = 16
S5_ROW = S5_CHUNK * S5_GROUP

LOG2E = math.log2(math.e)
SB_DEAD_LOG2 = 110.0 * LOG2E

VMEM_LIMIT = 56 * 1024 * 1024


def _cparams(sem):
    return pltpu.CompilerParams(dimension_semantics=sem, vmem_limit_bytes=VMEM_LIMIT)


def _resident(shape):
    nd = len(shape)
    return pl.BlockSpec(shape, lambda *_: (0,) * nd, pipeline_mode=pl.Buffered(1))


def _silu(x):
    return x * jax.nn.sigmoid(x)


def _dot(a, b):
    return jnp.dot(a, b, preferred_element_type=F32)


def _ada_kernel(c_ref, w_ref, b_ref, o_ref):
    s = _silu(c_ref[...])
    o_ref[0] = jnp.dot(s, w_ref[0], preferred_element_type=F32,
                       precision=lax.Precision.HIGHEST) + b_ref[0]


def _ada(c, ada_w, ada_b):
    depth, d, d3 = ada_w.shape
    bsz = c.shape[0]
    rows = 8
    c_pad = jnp.zeros((rows, d), F32).at[:bsz].set(c)
    tn = 1024
    out = pl.pallas_call(
        _ada_kernel,
        grid=(depth, d3 // tn),
        in_specs=[
            pl.BlockSpec((rows, d), lambda l, n: (0, 0)),
            pl.BlockSpec((1, d, tn), lambda l, n: (l, 0, n)),
            pl.BlockSpec((1, 1, tn), lambda l, n: (l, 0, n)),
        ],
        out_specs=pl.BlockSpec((1, rows, tn), lambda l, n: (l, 0, n)),
        out_shape=jax.ShapeDtypeStruct((depth, rows, d3), F32),
        compiler_params=_cparams(("arbitrary", "arbitrary")),
        name="ada",
    )(c_pad, ada_w, ada_b.reshape(depth, 1, d3))
    return out[:, :bsz]


def _norm_modulate(x, g, mod, d):
    ms = jnp.mean(x * x, axis=-1, keepdims=True)
    y = x * lax.rsqrt(ms + EPS) * g
    return y * (1.0 + mod[:, d:2 * d]) + mod[:, :d]


def _head_rms(t, g_row, scale):
    outs = []
    for h in range(t.shape[1] // SB_HEAD_DIM):
        th = t[:, h * SB_HEAD_DIM:(h + 1) * SB_HEAD_DIM]
        ms = jnp.mean(th * th, axis=-1, keepdims=True)
        outs.append(th * lax.rsqrt(ms + EPS) * (g_row * scale))
    return outs


def _in_even_kernel(x_ref, mod_ref, g_ref, w_ref, qg_ref, kg_ref,
                    xa_ref, ga_ref, q_ref, k_ref, v_ref, gb_ref):
    d = x_ref.shape[-1]
    h = _norm_modulate(x_ref[0], g_ref[...], mod_ref[0], d).astype(BF16)
    xa_ref[0] = _dot(h, w_ref[:, 0 * d:1 * d])
    ga_ref[0] = _dot(h, w_ref[:, 1 * d:2 * d]).astype(BF16)
    q = _dot(h, w_ref[:, 2 * d:3 * d])
    for i, qh in enumerate(_head_rms(q, qg_ref[...], LOG2E * SB_HEAD_DIM ** -0.5)):
        q_ref[0, :, i * SB_HEAD_DIM:(i + 1) * SB_HEAD_DIM] = qh.astype(BF16)
    k = _dot(h, w_ref[:, 3 * d:4 * d])
    for i, kh in enumerate(_head_rms(k, kg_ref[...], 1.0)):
        k_ref[0, :, i * SB_HEAD_DIM:(i + 1) * SB_HEAD_DIM] = kh.astype(BF16)
    v_ref[0] = _dot(h, w_ref[:, 4 * d:5 * d]).astype(BF16)
    gb_ref[0] = _dot(h, w_ref[:, 5 * d:6 * d]).astype(BF16)


def _in_even(x, mod, norm_g, w_in, q_g, k_g, tm):
    bsz, slen, d = x.shape
    tok = pl.BlockSpec((1, tm, d), lambda b, i: (b, i, 0))
    shp = lambda dt: jax.ShapeDtypeStruct((bsz, slen, d), dt)
    return pl.pallas_call(
        _in_even_kernel,
        grid=(bsz, slen // tm),
        in_specs=[
            tok,
            pl.BlockSpec((1, 1, 3 * d), lambda b, i: (b, 0, 0)),
            _resident((1, d)),
            _resident(w_in.shape),
            _resident((1, SB_HEAD_DIM)),
            _resident((1, SB_HEAD_DIM)),
        ],
        out_specs=[tok] * 6,
        out_shape=[shp(F32), shp(BF16), shp(BF16), shp(BF16), shp(BF16), shp(BF16)],
        compiler_params=_cparams(("arbitrary", "arbitrary")),
        name="in_even",
    )(x, mod, norm_g.reshape(1, d), w_in.astype(BF16), q_g.reshape(1, -1), k_g.reshape(1, -1))


def _scan_rows8(a, b):
    sub = lax.broadcasted_iota(jnp.int32, a.shape, 1)
    for d in (1, 2, 4):
        valid = sub >= d
        a_prev = jnp.where(valid, pltpu.roll(a, d, 1), 1.0)
        b_prev = jnp.where(valid, pltpu.roll(b, d, 1), 0.0)
        b = a * b_prev + b
        a = a * a_prev
    return a, b


def _lru_kernel(xa_ref, ga_ref, cw_ref, cb_ref, wr_ref, br_ref, wi_ref, bi_ref, lam_ref,
                o_ref, tail_ref, hbuf_ref, h_ref):
    ts, w = xa_ref.shape[1], xa_ref.shape[2]
    blk = w // LRU_HEADS

    @pl.when(pl.program_id(1) == 0)
    def _():
        tail_ref[...] = jnp.zeros_like(tail_ref)
        h_ref[...] = jnp.zeros_like(h_ref)

    x = xa_ref[0]
    tail = tail_ref[...]
    sub = lax.broadcasted_iota(jnp.int32, (8, w), 0)
    xc = cb_ref[...] + cw_ref[CONV_WIDTH - 1:CONV_WIDTH, :] * x
    for back in range(1, CONV_WIDTH):
        xs = pltpu.roll(x, back, 0)
        top = jnp.where(sub < back, pltpu.roll(tail, back, 0), xs[:8])
        xs = jnp.concatenate([top, xs[8:]], axis=0)
        xc = xc + cw_ref[CONV_WIDTH - 1 - back:CONV_WIDTH - back, :] * xs
    tail_ref[...] = x[ts - 8:]

    xb = xc.astype(BF16)
    rs, is_ = [], []
    for hd in range(LRU_HEADS):
        xh = xb[:, hd * blk:(hd + 1) * blk]
        rs.append(_dot(xh, wr_ref[hd]))
        is_.append(_dot(xh, wi_ref[hd]))
    r = jax.nn.sigmoid(jnp.concatenate(rs, axis=1) + br_ref[...])
    ig = jax.nn.sigmoid(jnp.concatenate(is_, axis=1) + bi_ref[...])
    lam = lam_ref[...]
    log_sig_lam = jnp.minimum(lam, 0.0) - jnp.log(1.0 + jnp.exp(-jnp.abs(lam)))
    log_a = r * (LRU_C * log_sig_lam)
    a = jnp.exp(log_a)
    v = -jnp.tanh(log_a) * (a * a + 1.0)
    b = (v * lax.rsqrt(jnp.maximum(v, 1e-30))) * (ig * xc)

    groups = ts // 8
    a3, b3 = _scan_rows8(a.reshape(groups, 8, w), b.reshape(groups, 8, w))
    carry = h_ref[7:8, :]
    hg = None
    for gi in range(groups):
        hg = a3[gi] * carry + b3[gi]
        hbuf_ref[8 * gi:8 * gi + 8, :] = hg
        carry = hg[7:8, :]
    h_ref[...] = hg
    o_ref[0] = (hbuf_ref[...] * _silu(ga_ref[0].astype(F32))).astype(BF16)


def _lru(xa, ga, conv_w, conv_b, wr, br, wi, bi, lam, ts):
    bsz, slen, w = xa.shape
    tok = pl.BlockSpec((1, ts, w), lambda b, i: (b, i, 0))
    row = lambda v: v.reshape(1, w)
    return pl.pallas_call(
        _lru_kernel,
        grid=(bsz, slen // ts),
        in_specs=[tok, tok, _resident(conv_w.shape), _resident((1, w)),
                  _resident(wr.shape), _resident((1, w)), _resident(wi.shape), _resident((1, w)),
                  _resident((1, w))],
        out_specs=tok,
        out_shape=jax.ShapeDtypeStruct((bsz, slen, w), BF16),
        scratch_shapes=[pltpu.VMEM((8, w), F32), pltpu.VMEM((ts, w), F32),
                        pltpu.VMEM((8, w), F32)],
        compiler_params=_cparams(("arbitrary", "arbitrary")),
        name="lru",
    )(xa, ga, conv_w, row(conv_b), wr.astype(BF16), row(br), wi.astype(BF16), row(bi), row(lam))


def _sb_kernel(q_ref, k_ref, v_ref, gb_ref, o_ref, acc_ref, carry_ref):
    tq = q_ref.shape[1]
    dh = SB_HEAD_DIM
    heads = q_ref.shape[2] // dh
    row = lax.broadcasted_iota(jnp.int32, (tq, tq), 0)
    col = lax.broadcasted_iota(jnp.int32, (tq, tq), 1)
    after = jnp.where(row > col, 1.0, 0.0).astype(BF16)
    ones = jnp.ones((tq, LANES), BF16)
    causal = col < row

    def scores(hd, start, diag):
        lanes = slice(hd * dh, (hd + 1) * dh)
        kblk = k_ref[0, pl.ds(start, tq), lanes]
        z = lax.dot_general(q_ref[0, :, lanes], kblk, (((1,), (1,)), ((), ())),
                            preferred_element_type=F32)
        mx = jnp.maximum(z, 0.0)
        mn = jnp.minimum(z, 0.0)
        l = jnp.log2(1.0 + jnp.exp2(mn - mx))
        sp = l + mx
        if diag:
            sp = jnp.where(causal, sp, 0.0)
        sp = sp.astype(BF16)
        rowsum = _dot(sp, ones)
        old = None if diag else carry_ref[hd]
        carry_ref[hd] = rowsum if diag else old + rowsum
        return sp, mn - l, old

    def weights(hd, start, diag, sp, log_beta, old):
        lanes = slice(hd * dh, (hd + 1) * dh)
        vblk = v_ref[0, pl.ds(start, tq), lanes]
        later = _dot(sp, after)
        if diag:
            wgt = jnp.where(causal, jnp.exp2(log_beta - later), 0.0)
            acc_ref[hd] = _dot(wgt.astype(BF16), vblk)
        else:
            later = later + jnp.concatenate([old] * (tq // LANES), axis=1)
            acc_ref[hd] += _dot(jnp.exp2(log_beta - later).astype(BF16), vblk)

    def sweep(start, diag):
        st = [scores(hd, start, diag) for hd in range(heads)]
        m = carry_ref[0]
        for hd in range(1, heads):
            m = jnp.minimum(m, carry_ref[hd])
        alive = (jnp.min(m) < SB_DEAD_LOG2).astype(jnp.int32)
        for hd in range(heads):
            weights(hd, start, diag, *st[hd])
        return alive

    q0 = pl.multiple_of(pl.program_id(2) * tq, tq)

    def cond(st):
        end, alive = st
        return jnp.logical_and(end > 0, alive > 0)

    def body(st):
        start = pl.multiple_of(st[0] - tq, tq)
        return start, sweep(start, False)

    lax.while_loop(cond, body, (q0, sweep(q0, True)))
    for hd in range(heads):
        lanes = slice(hd * dh, (hd + 1) * dh)
        o_ref[0, :, lanes] = (acc_ref[hd] * _silu(gb_ref[0, :, lanes].astype(F32))).astype(BF16)


def _sb_attn(q, k, v, gb, tq, heads_per_step):
    bsz, slen, w = q.shape
    wd = heads_per_step * SB_HEAD_DIM
    qspec = pl.BlockSpec((1, tq, wd), lambda b, h, i: (b, i, h))
    kvspec = pl.BlockSpec((1, slen, wd), lambda b, h, i: (b, 0, h), pipeline_mode=pl.Buffered(1))
    return pl.pallas_call(
        _sb_kernel,
        grid=(bsz, w // wd, slen // tq),
        in_specs=[qspec, kvspec, kvspec, qspec],
        out_specs=qspec,
        out_shape=jax.ShapeDtypeStruct((bsz, slen, w), BF16),
        scratch_shapes=[pltpu.VMEM((heads_per_step, tq, SB_HEAD_DIM), F32),
                        pltpu.VMEM((heads_per_step, tq, LANES), F32)],
        compiler_params=_cparams(("arbitrary", "arbitrary", "arbitrary")),
        name="sb_attn",
    )(q, k, v, gb)


def _out_even_kernel(x_ref, ya_ref, yb_ref, mod_ref, w_ref, o_ref):
    d = x_ref.shape[-1]
    wa = ya_ref.shape[-1]
    out = _dot(ya_ref[0], w_ref[:wa, :]) + _dot(yb_ref[0], w_ref[wa:, :])
    o_ref[0] = x_ref[0] + mod_ref[0][:, 2 * d:] * out


def _out_even(x, ya, yb, mod, w_out, tm):
    bsz, slen, d = x.shape
    tok = lambda wd: pl.BlockSpec((1, tm, wd), lambda b, i: (b, i, 0))
    return pl.pallas_call(
        _out_even_kernel,
        grid=(bsz, slen // tm),
        in_specs=[tok(d), tok(ya.shape[-1]), tok(yb.shape[-1]),
                  pl.BlockSpec((1, 1, 3 * d), lambda b, i: (b, 0, 0)),
                  _resident(w_out.shape)],
        out_specs=tok(d),
        out_shape=jax.ShapeDtypeStruct((bsz, slen, d), F32),
        compiler_params=_cparams(("arbitrary", "arbitrary")),
        name="out_even",
    )(x, ya, yb, mod, w_out.astype(BF16))


def _chunk_major_perm(tm):
    rt = tm // S5_CHUNK
    i = jnp.arange(tm)
    src = S5_CHUNK * (i % rt) + i // rt
    return (src[:, None] == jnp.arange(tm)[None, :]).astype(BF16)


def _lane_block_transpose(src, put):
    per_tile = LANES // S5_GROUP
    lane_blk = lax.broadcasted_iota(jnp.int32, src[0].shape, 1) // S5_GROUP
    rolled = []
    for d in range(per_tile):
        t = src[d]
        for a in range(1, per_tile):
            t = jnp.where(lane_blk == a, src[(a + d) % per_tile], t)
        rolled.append(pltpu.roll(t, S5_GROUP * d, 1) if d else t)
    for a in range(per_tile):
        out = rolled[(-a) % per_tile]
        for b in range(1, per_tile):
            out = jnp.where(lane_blk == b, rolled[(b - a) % per_tile], out)
        put(a, out)


def _bands_to_s5_rows(u, ug_ref):
    rt = u.shape[0] // S5_CHUNK
    per_tile = LANES // S5_GROUP
    for q in range(u.shape[1] // LANES):
        for m in range(S5_ROW // LANES):
            bands = [u[(m * per_tile + jj) * rt:(m * per_tile + jj + 1) * rt,
                       q * LANES:(q + 1) * LANES] for jj in range(per_tile)]

            def put(gl, rows, q=q, m=m):
                ug_ref[q * per_tile + gl, 0, :, m * LANES:(m + 1) * LANES] = rows.astype(BF16)

            _lane_block_transpose(bands, put)


def _s5_rows_to_bands(yg_ref, ys_ref):
    rt = ys_ref.shape[0] // S5_CHUNK
    per_tile = LANES // S5_GROUP
    for q in range(ys_ref.shape[1] // LANES):
        for m in range(S5_ROW // LANES):
            rows = [yg_ref[q * per_tile + gl, 0, :, m * LANES:(m + 1) * LANES].astype(F32)
                    for gl in range(per_tile)]

            def put(jj, band, q=q, m=m):
                j = m * per_tile + jj
                ys_ref[j * rt:(j + 1) * rt, q * LANES:(q + 1) * LANES] = band

            _lane_block_transpose(rows, put)


def _in_odd_kernel(x_ref, mod_ref, g_ref, perm_ref, w_ref, ug_ref, gate_ref):
    d = x_ref.shape[-1]
    wu = w_ref.shape[1] // 2
    h = _norm_modulate(x_ref[0], g_ref[...], mod_ref[0], d).astype(BF16)
    hp = _dot(perm_ref[...], h).astype(BF16)
    gate_ref[0] = _dot(hp, w_ref[:, wu:]).astype(BF16)
    _bands_to_s5_rows(_dot(hp, w_ref[:, :wu]), ug_ref)


def _in_odd(x, mod, norm_g, w_in, tm):
    bsz, slen, d = x.shape
    wu = w_in.shape[1] // 2
    groups = wu // S5_GROUP
    rt = tm // S5_CHUNK
    tok = lambda wd: pl.BlockSpec((1, tm, wd), lambda b, i: (b, i, 0))
    return pl.pallas_call(
        _in_odd_kernel,
        grid=(bsz, slen // tm),
        in_specs=[tok(d), pl.BlockSpec((1, 1, 3 * d), lambda b, i: (b, 0, 0)),
                  _resident((1, d)), _resident((tm, tm)), _resident(w_in.shape)],
        out_specs=[pl.BlockSpec((groups, 1, rt, S5_ROW), lambda b, i: (0, b, i, 0)), tok(wu)],
        out_shape=[jax.ShapeDtypeStruct((groups, bsz, slen // S5_CHUNK, S5_ROW), BF16),
                   jax.ShapeDtypeStruct((bsz, slen, wu), BF16)],
        compiler_params=_cparams(("arbitrary", "arbitrary")),
        name="in_odd",
    )(x, mod, norm_g.reshape(1, d), _chunk_major_perm(tm), w_in.astype(BF16))


def _s5_weights(lam_re, lam_im, log_dt, b_re, b_im, c_re, c_im):
    hp = lax.Precision.HIGHEST
    L = S5_CHUNK
    groups = lam_re.shape[0]
    dt = jnp.exp(log_dt.astype(F32))[:, None]
    lam_re = lam_re.astype(F32)
    lam_im = lam_im.astype(F32)
    decay = jnp.exp(lam_re * dt)
    ang = lam_im * dt
    abar_re = decay * jnp.cos(ang)
    abar_im = decay * jnp.sin(ang)
    den = lam_re * lam_re + lam_im * lam_im
    num_re = abar_re - 1.0
    coef_re = (num_re * lam_re + abar_im * lam_im) / den
    coef_im = (abar_im * lam_re - num_re * lam_im) / den
    b_re = b_re.astype(F32)
    b_im = b_im.astype(F32)
    bbar_re = coef_re[..., None] * b_re - coef_im[..., None] * b_im
    bbar_im = coef_re[..., None] * b_im + coef_im[..., None] * b_re
    c_re = c_re.astype(F32)
    c_im = c_im.astype(F32)

    def power(n):
        n = n.astype(F32)[None, :, None]
        mag = jnp.exp(n * (lam_re * dt)[:, None, :])
        arg = n * ang[:, None, :]
        return mag * jnp.cos(arg), mag * jnp.sin(arg)

    pw_re, pw_im = power(jnp.arange(L + 1))
    ct_re = c_re.transpose(0, 2, 1)[:, :, None, :]
    ct_im = c_im.transpose(0, 2, 1)[:, :, None, :]
    pt_re = pw_re.transpose(0, 2, 1)[..., None]
    pt_im = pw_im.transpose(0, 2, 1)[..., None]
    ca_re = ct_re * pt_re - ct_im * pt_im
    ca_im = ct_re * pt_im + ct_im * pt_re
    bt_re = bbar_re.transpose(0, 2, 1)
    bt_im = bbar_im.transpose(0, 2, 1)
    strip = jnp.einsum(
        "gip,gpk->gik", jnp.concatenate([bt_re, -bt_im], axis=-1),
        jnp.concatenate([ca_re[:, :, :L], ca_im[:, :, :L]], axis=1).reshape(groups, 2 * S5_STATE, S5_ROW),
        precision=hp)
    padded = jnp.concatenate([jnp.zeros_like(strip), strip], axis=-1)
    toep = jnp.stack([padded[:, :, S5_ROW - S5_GROUP * ji:2 * S5_ROW - S5_GROUP * ji]
                      for ji in range(L)], axis=1).reshape(groups, S5_ROW, S5_ROW)

    rev_re = pw_re[:, L - 1::-1][:, :L, None, :]
    rev_im = pw_im[:, L - 1::-1][:, :L, None, :]
    bq_re, bq_im = bt_re[:, None], bt_im[:, None]
    bp_re = (rev_re * bq_re - rev_im * bq_im).reshape(groups, S5_ROW, S5_STATE)
    bp_im = (rev_re * bq_im + rev_im * bq_re).reshape(groups, S5_ROW, S5_STATE)
    zb = jnp.zeros_like(bp_re)
    bsel = jnp.concatenate([jnp.concatenate([bp_re, zb, bp_im, zb], -1),
                            jnp.concatenate([zb, bp_re, zb, bp_im], -1)], axis=1)

    cp_re = ca_re[:, :, 1:].reshape(groups, S5_STATE, S5_ROW)
    cp_im = -ca_im[:, :, 1:].reshape(groups, S5_STATE, S5_ROW)
    zc = jnp.zeros_like(cp_re)
    csel = jnp.concatenate([jnp.concatenate([cp_re, zc], -1), jnp.concatenate([zc, cp_re], -1),
                            jnp.concatenate([cp_im, zc], -1), jnp.concatenate([zc, cp_im], -1)], axis=1)

    def both_halves(re, im):
        return jnp.stack([jnp.concatenate([re, re], -1), jnp.concatenate([im, im], -1)], axis=2)

    lvl = both_halves(*power(L * jnp.array([1, 2, 4])))
    pw8 = both_halves(*power(L * jnp.arange(1, 9))).transpose(0, 2, 1, 3)
    return toep, bsel, csel, lvl, pw8


def _s5_kernel(u_ref, toep_ref, bsel_ref, csel_ref, lvl_ref, pw8_ref, d_ref, y_ref,
               hre_ref, him_ref):
    bsz, chunks, width = u_ref.shape[1], u_ref.shape[2], u_ref.shape[3]
    lanes = lvl_ref.shape[-1]
    groups8 = chunks // 8
    row = lax.broadcasted_iota(jnp.int32, (chunks, lanes), 0)
    sub = lax.broadcasted_iota(jnp.int32, (groups8, 8, lanes), 1)
    toep = toep_ref[0]
    pw_re, pw_im = pw8_ref[0, 0], pw8_ref[0, 1]
    for pair in range(bsz // 2):
        us = [u_ref[0, 2 * pair + s] for s in range(2)]
        x = _dot(jnp.concatenate(us, axis=1), bsel_ref[0])
        xr = jnp.where(row >= 1, pltpu.roll(x[:, :lanes], 1, 0), 0.0).reshape(groups8, 8, lanes)
        xi = jnp.where(row >= 1, pltpu.roll(x[:, lanes:], 1, 0), 0.0).reshape(groups8, 8, lanes)
        for k, d in enumerate((1, 2, 4)):
            cr, ci = lvl_ref[0, k, 0:1, :], lvl_ref[0, k, 1:2, :]
            pr = jnp.where(sub >= d, pltpu.roll(xr, d, 1), 0.0)
            pi = jnp.where(sub >= d, pltpu.roll(xi, d, 1), 0.0)
            xr, xi = xr + (cr * pr - ci * pi), xi + (cr * pi + ci * pr)
        car = jnp.zeros((1, lanes), F32)
        cai = jnp.zeros((1, lanes), F32)
        for g in range(groups8):
            hr = xr[g] + (pw_re * car - pw_im * cai)
            hi = xi[g] + (pw_re * cai + pw_im * car)
            hre_ref[pair, 8 * g:8 * g + 8, :] = hr
            him_ref[pair, 8 * g:8 * g + 8, :] = hi
            car, cai = hr[7:8, :], hi[7:8, :]
        h = jnp.concatenate([hre_ref[pair], him_ref[pair]], axis=1).astype(BF16)
        y = _dot(h, csel_ref[0])
        for s in range(2):
            ys = y[:, s * width:(s + 1) * width] + _dot(us[s], toep)
            ys = ys + d_ref[0] * us[s].astype(F32)
            y_ref[0, 2 * pair + s] = ys.astype(y_ref.dtype)


def _s5(ug, toep, bsel, csel, lvl, pw8, d_rows):
    groups, bsz, chunks, width = ug.shape
    assert bsz % 2 == 0 and chunks % 8 == 0
    gspec = lambda a: pl.BlockSpec((1,) + a.shape[1:], lambda g: (g,) + (0,) * (a.ndim - 1))
    tile = pl.BlockSpec((1, bsz, chunks, width), lambda g: (g, 0, 0, 0))
    state = pltpu.VMEM((bsz // 2, chunks, lvl.shape[-1]), F32)
    return pl.pallas_call(
        _s5_kernel,
        grid=(groups,),
        in_specs=[tile, gspec(toep), gspec(bsel), gspec(csel), gspec(lvl), gspec(pw8),
                  gspec(d_rows)],
        out_specs=tile,
        out_shape=jax.ShapeDtypeStruct(ug.shape, BF16),
        scratch_shapes=[state, state],
        compiler_params=_cparams(("arbitrary",)),
        name="s5",
    )(ug, toep.astype(BF16), bsel.astype(BF16), csel.astype(BF16), lvl, pw8, d_rows)


def _gelu_tanh(x):
    return 0.5 * x * (1.0 + jnp.tanh(math.sqrt(2.0 / math.pi) * (x + 0.044715 * (x * x * x))))


def _out_odd_kernel(x_ref, yg_ref, gate_ref, mod_ref, gw_ref, gb_ref, unperm_ref, w_ref,
                    o_ref, ys_ref):
    d = x_ref.shape[-1]
    _s5_rows_to_bands(yg_ref, ys_ref)
    y = _gelu_tanh(ys_ref[...])
    y = y * jax.nn.sigmoid(_dot(y.astype(BF16), gw_ref[...]) + gb_ref[...])
    y = (y * _silu(gate_ref[0].astype(F32))).astype(BF16)
    y = _dot(unperm_ref[...], y).astype(BF16)
    o_ref[0] = x_ref[0] + mod_ref[0][:, 2 * d:] * _dot(y, w_ref[...])


def _out_odd(x, yg, gate, mod, glu_w, glu_b, w_out, tm):
    bsz, slen, d = x.shape
    groups = yg.shape[0]
    w = gate.shape[-1]
    rt = tm // S5_CHUNK
    tok = lambda wd: pl.BlockSpec((1, tm, wd), lambda b, i: (b, i, 0))
    return pl.pallas_call(
        _out_odd_kernel,
        grid=(bsz, slen // tm),
        in_specs=[tok(d), pl.BlockSpec((groups, 1, rt, S5_ROW), lambda b, i: (0, b, i, 0)), tok(w),
                  pl.BlockSpec((1, 1, 3 * d), lambda b, i: (b, 0, 0)),
                  _resident(glu_w.shape), _resident((1, w)), _resident((tm, tm)),
                  _resident(w_out.shape)],
        out_specs=tok(d),
        out_shape=jax.ShapeDtypeStruct((bsz, slen, d), F32),
        scratch_shapes=[pltpu.VMEM((tm, w), F32)],
        compiler_params=_cparams(("arbitrary", "arbitrary")),
        name="out_odd",
    )(x, yg, gate, mod, glu_w.astype(BF16), glu_b.reshape(1, w), _chunk_major_perm(tm).T,
      w_out.astype(BF16))


def _even_layer(x, mod, norm_g, w_in, conv_w, conv_b, wr, br, wi, bi, lam, q_g, k_g, w_out):
    xa, ga, q, k, v, gb = _in_even(x, mod, norm_g, w_in, q_g, k_g, tm=512)
    ya = _lru(xa, ga, conv_w, conv_b, wr, br, wi, bi, lam, ts=256)
    yb = _sb_attn(q, k, v, gb, tq=256, heads_per_step=8)
    return _out_even(x, ya, yb, mod, w_out, tm=512)


def _odd_layer(x, mod, norm_g, w_in, lam_re, lam_im, log_dt, b_re, b_im, c_re, c_im, d_skip,
               glu_w, glu_b, w_out):
    tm = 512
    ug, gate = _in_odd(x, mod, norm_g, w_in, tm=tm)
    groups = ug.shape[0]
    weights = _s5_weights(lam_re, lam_im, log_dt, b_re, b_im, c_re, c_im)
    d_rows = jnp.tile(d_skip.astype(F32).reshape(groups, 1, S5_GROUP), (1, 1, S5_CHUNK))
    yg = _s5(ug, *weights, d_rows)
    return _out_odd(x, yg, gate, mod, glu_w, glu_b, w_out, tm=tm)


def kernel(x, c, norm_g, ada_w, ada_b, w_in_even, conv_w, conv_b, lru_wr, lru_br, lru_wi, lru_bi,
           lru_lambda, q_norm_g, k_norm_g, w_out_even, w_in_odd, s5_lambda_re, s5_lambda_im,
           s5_log_dt, s5_b_re, s5_b_im, s5_c_re, s5_c_im, s5_d, glu_w, glu_b, w_out_odd):
    depth = norm_g.shape[0]
    mods = _ada(c, ada_w, ada_b)
    for layer in range(depth):
        mod = mods[layer][:, None, :]
        j = layer // 2
        if layer % 2 == 0:
            x = _even_layer(x, mod, norm_g[layer], w_in_even[j], conv_w[j], conv_b[j], lru_wr[j],
                            lru_br[j], lru_wi[j], lru_bi[j], lru_lambda[j], q_norm_g[j],
                            k_norm_g[j], w_out_even[j])
        else:
            x = _odd_layer(x, mod, norm_g[layer], w_in_odd[j], s5_lambda_re[j], s5_lambda_im[j],
                           s5_log_dt[j], s5_b_re[j], s5_b_im[j], s5_c_re[j], s5_c_im[j], s5_d[j],
                           glu_w[j], glu_b[j], w_out_odd[j])
    return x
```

```python
import math

import jax
import jax.numpy as jnp
from jax import lax
from jax.experimental import pallas as pl
from jax.experimental.pallas import tpu as pltpu

F32 = jnp.float32
BF16 = jnp.bfloat16

EPS = 1e-6
LANES = 128
LRU_HEADS = 8
LRU_C = 8.0
CONV_WIDTH = 4
SB_HEAD_DIM = 128
S5_GROUP = 16
S5_STATE = 64
S5_CHUNK = 16
S5_ROW = S5_CHUNK * S5_GROUP

LOG2E = math.log2(math.e)
SB_DEAD_LOG2 = 110.0 * LOG2E

VMEM_LIMIT = 56 * 1024 * 1024


def _cparams(sem):
    return pltpu.CompilerParams(dimension_semantics=sem, vmem_limit_bytes=VMEM_LIMIT)


def _resident(shape):
    nd = len(shape)
    return pl.BlockSpec(shape, lambda *_: (0,) * nd, pipeline_mode=pl.Buffered(1))


def _silu(x):
    return x * jax.nn.sigmoid(x)


def _dot(a, b):
    return jnp.dot(a, b, preferred_element_type=F32)


def _ada_kernel(c_ref, w_ref, b_ref, o_ref):
    s = _silu(c_ref[...])
    o_ref[0] = jnp.dot(s, w_ref[0], preferred_element_type=F32,
                       precision=lax.Precision.HIGHEST) + b_ref[0]


def _ada(c, ada_w, ada_b):
    depth, d, d3 = ada_w.shape
    bsz = c.shape[0]
    rows = 8
    c_pad = jnp.zeros((rows, d), F32).at[:bsz].set(c)
    tn = 1024
    out = pl.pallas_call(
        _ada_kernel,
        grid=(depth, d3 // tn),
        in_specs=[
            pl.BlockSpec((rows, d), lambda l, n: (0, 0)),
            pl.BlockSpec((1, d, tn), lambda l, n: (l, 0, n)),
            pl.BlockSpec((1, 1, tn), lambda l, n: (l, 0, n)),
        ],
        out_specs=pl.BlockSpec((1, rows, tn), lambda l, n: (l, 0, n)),
        out_shape=jax.ShapeDtypeStruct((depth, rows, d3), F32),
        compiler_params=_cparams(("arbitrary", "arbitrary")),
        name="ada",
    )(c_pad, ada_w, ada_b.reshape(depth, 1, d3))
    return out[:, :bsz]


def _norm_modulate(x, g, mod, d):
    ms = jnp.mean(x * x, axis=-1, keepdims=True)
    y = x * lax.rsqrt(ms + EPS) * g
    return y * (1.0 + mod[:, d:2 * d]) + mod[:, :d]


def _head_rms(t, g_row, scale):
    outs = []
    for h in range(t.shape[1] // SB_HEAD_DIM):
        th = t[:, h * SB_HEAD_DIM:(h + 1) * SB_HEAD_DIM]
        ms = jnp.mean(th * th, axis=-1, keepdims=True)
        outs.append(th * lax.rsqrt(ms + EPS) * (g_row * scale))
    return outs


def _in_even_kernel(x_ref, mod_ref, g_ref, w_ref, qg_ref, kg_ref,
                    xa_ref, ga_ref, q_ref, k_ref, v_ref, gb_ref):
    d = x_ref.shape[-1]
    h = _norm_modulate(x_ref[0], g_ref[...], mod_ref[0], d).astype(BF16)
    xa_ref[0] = _dot(h, w_ref[:, 0 * d:1 * d])
    ga_ref[0] = _dot(h, w_ref[:, 1 * d:2 * d]).astype(BF16)
    q = _dot(h, w_ref[:, 2 * d:3 * d])
    for i, qh in enumerate(_head_rms(q, qg_ref[...], LOG2E * SB_HEAD_DIM ** -0.5)):
        q_ref[0, :, i * SB_HEAD_DIM:(i + 1) * SB_HEAD_DIM] = qh.astype(BF16)
    k = _dot(h, w_ref[:, 3 * d:4 * d])
    for i, kh in enumerate(_head_rms(k, kg_ref[...], 1.0)):
        k_ref[0, :, i * SB_HEAD_DIM:(i + 1) * SB_HEAD_DIM] = kh.astype(BF16)
    v_ref[0] = _dot(h, w_ref[:, 4 * d:5 * d]).astype(BF16)
    gb_ref[0] = _dot(h, w_ref[:, 5 * d:6 * d]).astype(BF16)


def _in_even(x, mod, norm_g, w_in, q_g, k_g, tm):
    bsz, slen, d = x.shape
    tok = pl.BlockSpec((1, tm, d), lambda b, i: (b, i, 0))
    shp = lambda dt: jax.ShapeDtypeStruct((bsz, slen, d), dt)
    return pl.pallas_call(
        _in_even_kernel,
        grid=(bsz, slen // tm),
        in_specs=[
            tok,
            pl.BlockSpec((1, 1, 3 * d), lambda b, i: (b, 0, 0)),
            _resident((1, d)),
            _resident(w_in.shape),
            _resident((1, SB_HEAD_DIM)),
            _resident((1, SB_HEAD_DIM)),
        ],
        out_specs=[tok] * 6,
        out_shape=[shp(F32), shp(BF16), shp(BF16), shp(BF16), shp(BF16), shp(BF16)],
        compiler_params=_cparams(("arbitrary", "arbitrary")),
        name="in_even",
    )(x, mod, norm_g.reshape(1, d), w_in.astype(BF16), q_g.reshape(1, -1), k_g.reshape(1, -1))


def _scan_rows8(a, b):
    sub = lax.broadcasted_iota(jnp.int32, a.shape, 1)
    for d in (1, 2, 4):
        valid = sub >= d
        a_prev = jnp.where(valid, pltpu.roll(a, d, 1), 1.0)
        b_prev = jnp.where(valid, pltpu.roll(b, d, 1), 0.0)
        b = a * b_prev + b
        a = a * a_prev
    return a, b


def _lru_kernel(xa_ref, ga_ref, cw_ref, cb_ref, wr_ref, br_ref, wi_ref, bi_ref, lam_ref,
                o_ref, tail_ref, hbuf_ref, h_ref):
    ts, w = xa_ref.shape[1], xa_ref.shape[2]
    blk = w // LRU_HEADS

    @pl.when(pl.program_id(1) == 0)
    def _():
        tail_ref[...] = jnp.zeros_like(tail_ref)
        h_ref[...] = jnp.zeros_like(h_ref)

    x = xa_ref[0]
    tail = tail_ref[...]
    sub = lax.broadcasted_iota(jnp.int32, (8, w), 0)
    xc = cb_ref[...] + cw_ref[CONV_WIDTH - 1:CONV_WIDTH, :] * x
    for back in range(1, CONV_WIDTH):
        xs = pltpu.roll(x, back, 0)
        top = jnp.where(sub < back, pltpu.roll(tail, back, 0), xs[:8])
        xs = jnp.concatenate([top, xs[8:]], axis=0)
        xc = xc + cw_ref[CONV_WIDTH - 1 - back:CONV_WIDTH - back, :] * xs
    tail_ref[...] = x[ts - 8:]

    xb = xc.astype(BF16)
    rs, is_ = [], []
    for hd in range(LRU_HEADS):
        xh = xb[:, hd * blk:(hd + 1) * blk]
        rs.append(_dot(xh, wr_ref[hd]))
        is_.append(_dot(xh, wi_ref[hd]))
    r = jax.nn.sigmoid(jnp.concatenate(rs, axis=1) + br_ref[...])
    ig = jax.nn.sigmoid(jnp.concatenate(is_, axis=1) + bi_ref[...])
    lam = lam_ref[...]
    log_sig_lam = jnp.minimum(lam, 0.0) - jnp.log(1.0 + jnp.exp(-jnp.abs(lam)))
    log_a = r * (LRU_C * log_sig_lam)
    a = jnp.exp(log_a)
    v = -jnp.tanh(log_a) * (a * a + 1.0)
    b = (v * lax.rsqrt(jnp.maximum(v, 1e-30))) * (ig * xc)

    groups = ts // 8
    a3, b3 = _scan_rows8(a.reshape(groups, 8, w), b.reshape(groups, 8, w))
    carry = h_ref[7:8, :]
    hg = None
    for gi in range(groups):
        hg = a3[gi] * carry + b3[gi]
        hbuf_ref[8 * gi:8 * gi + 8, :] = hg
        carry = hg[7:8, :]
    h_ref[...] = hg
    o_ref[0] = (hbuf_ref[...] * _silu(ga_ref[0].astype(F32))).astype(BF16)


def _lru(xa, ga, conv_w, conv_b, wr, br, wi, bi, lam, ts):
    bsz, slen, w = xa.shape
    tok = pl.BlockSpec((1, ts, w), lambda b, i: (b, i, 0))
    row = lambda v: v.reshape(1, w)
    return pl.pallas_call(
        _lru_kernel,
        grid=(bsz, slen // ts),
        in_specs=[tok, tok, _resident(conv_w.shape), _resident((1, w)),
                  _resident(wr.shape), _resident((1, w)), _resident(wi.shape), _resident((1, w)),
                  _resident((1, w))],
        out_specs=tok,
        out_shape=jax.ShapeDtypeStruct((bsz, slen, w), BF16),
        scratch_shapes=[pltpu.VMEM((8, w), F32), pltpu.VMEM((ts, w), F32),
                        pltpu.VMEM((8, w), F32)],
        compiler_params=_cparams(("arbitrary", "arbitrary")),
        name="lru",
    )(xa, ga, conv_w, row(conv_b), wr.astype(BF16), row(br), wi.astype(BF16), row(bi), row(lam))


def _sb_kernel(q_ref, k_ref, v_ref, gb_ref, o_ref, acc_ref, carry_ref):
    tq = q_ref.shape[1]
    dh = SB_HEAD_DIM
    heads = q_ref.shape[2] // dh
    row = lax.broadcasted_iota(jnp.int32, (tq, tq), 0)
    col = lax.broadcasted_iota(jnp.int32, (tq, tq), 1)
    after = jnp.where(row > col, 1.0, 0.0).astype(BF16)
    causal = col < row

    def scores(hd, start, diag):
        lanes = slice(hd * dh, (hd + 1) * dh)
        kblk = k_ref[0, pl.ds(start, tq), lanes]
        z = lax.dot_general(q_ref[0, :, lanes], kblk, (((1,), (1,)), ((), ())),
                            preferred_element_type=F32)
        mx = jnp.maximum(z, 0.0)
        mn = jnp.minimum(z, 0.0)
        l = jnp.log2(1.0 + jnp.exp2(mn - mx))
        sp = l + mx
        if diag:
            sp = jnp.where(causal, sp, 0.0)
        sp = sp.astype(BF16)
        later = _dot(sp, after)
        rowsum = jnp.broadcast_to(later[:, 0:1] + sp[:, 0:1].astype(F32), (tq, LANES))
        old = None if diag else carry_ref[hd]
        carry_ref[hd] = rowsum if diag else old + rowsum
        return later, mn - l, old

    def weights(hd, start, diag, later, log_beta, old):
        lanes = slice(hd * dh, (hd + 1) * dh)
        vblk = v_ref[0, pl.ds(start, tq), lanes]
        if diag:
            wgt = jnp.where(causal, jnp.exp2(log_beta - later), 0.0)
            acc_ref[hd] = _dot(wgt.astype(BF16), vblk)
        else:
            later = later + jnp.concatenate([old] * (tq // LANES), axis=1)
            acc_ref[hd] += _dot(jnp.exp2(log_beta - later).astype(BF16), vblk)

    def sweep(blocks):
        work = [(hd, start, diag) for start, diag in blocks for hd in range(heads)]
        st = [scores(*w) for w in work]
        m = carry_ref[0]
        for hd in range(1, heads):
            m = jnp.minimum(m, carry_ref[hd])
        alive = (jnp.min(m) < SB_DEAD_LOG2).astype(jnp.int32)
        for w, s in zip(work, st):
            weights(*w, *s)
        return alive

    q0 = pl.multiple_of(pl.program_id(2) * tq, tq)
    prev = pl.multiple_of(jnp.maximum(q0 - tq, 0), tq)
    alive = lax.cond(q0 > 0, lambda: sweep([(q0, True), (prev, False)]),
                     lambda: sweep([(q0, True)]))

    def cond(st):
        end, alive = st
        return jnp.logical_and(end > 0, alive > 0)

    def body(st):
        start = pl.multiple_of(st[0] - tq, tq)
        return start, sweep([(start, False)])

    lax.while_loop(cond, body, (prev, alive))
    for hd in range(heads):
        lanes = slice(hd * dh, (hd + 1) * dh)
        o_ref[0, :, lanes] = (acc_ref[hd] * _silu(gb_ref[0, :, lanes].astype(F32))).astype(BF16)


def _sb_attn(q, k, v, gb, tq, heads_per_step):
    bsz, slen, w = q.shape
    wd = heads_per_step * SB_HEAD_DIM
    qspec = pl.BlockSpec((1, tq, wd), lambda b, h, i: (b, i, h))
    kvspec = pl.BlockSpec((1, slen, wd), lambda b, h, i: (b, 0, h), pipeline_mode=pl.Buffered(1))
    return pl.pallas_call(
        _sb_kernel,
        grid=(bsz, w // wd, slen // tq),
        in_specs=[qspec, kvspec, kvspec, qspec],
        out_specs=qspec,
        out_shape=jax.ShapeDtypeStruct((bsz, slen, w), BF16),
        scratch_shapes=[pltpu.VMEM((heads_per_step, tq, SB_HEAD_DIM), F32),
                        pltpu.VMEM((heads_per_step, tq, LANES), F32)],
        compiler_params=_cparams(("arbitrary", "arbitrary", "arbitrary")),
        name="sb_attn",
    )(q, k, v, gb)


def _out_even_kernel(x_ref, ya_ref, yb_ref, mod_ref, w_ref, o_ref):
    d = x_ref.shape[-1]
    wa = ya_ref.shape[-1]
    out = _dot(ya_ref[0], w_ref[:wa, :]) + _dot(yb_ref[0], w_ref[wa:, :])
    o_ref[0] = x_ref[0] + mod_ref[0][:, 2 * d:] * out


def _out_even(x, ya, yb, mod, w_out, tm):
    bsz, slen, d = x.shape
    tok = lambda wd: pl.BlockSpec((1, tm, wd), lambda b, i: (b, i, 0))
    return pl.pallas_call(
        _out_even_kernel,
        grid=(bsz, slen // tm),
        in_specs=[tok(d), tok(ya.shape[-1]), tok(yb.shape[-1]),
                  pl.BlockSpec((1, 1, 3 * d), lambda b, i: (b, 0, 0)),
                  _resident(w_out.shape)],
        out_specs=tok(d),
        out_shape=jax.ShapeDtypeStruct((bsz, slen, d), F32),
        compiler_params=_cparams(("arbitrary", "arbitrary")),
        name="out_even",
    )(x, ya, yb, mod, w_out.astype(BF16))


def _chunk_major_perm(tm):
    rt = tm // S5_CHUNK
    i = jnp.arange(tm)
    src = S5_CHUNK * (i % rt) + i // rt
    return (src[:, None] == jnp.arange(tm)[None, :]).astype(BF16)


def _lane_block_transpose(src, put):
    per_tile = LANES // S5_GROUP
    lane_blk = lax.broadcasted_iota(jnp.int32, src[0].shape, 1) // S5_GROUP
    rolled = []
    for d in range(per_tile):
        t = src[d]
        for a in range(1, per_tile):
            t = jnp.where(lane_blk == a, src[(a + d) % per_tile], t)
        rolled.append(pltpu.roll(t, S5_GROUP * d, 1) if d else t)
    for a in range(per_tile):
        out = rolled[(-a) % per_tile]
        for b in range(1, per_tile):
            out = jnp.where(lane_blk == b, rolled[(b - a) % per_tile], out)
        put(a, out)


def _bands_to_s5_rows(u, ug_ref):
    rt = u.shape[0] // S5_CHUNK
    per_tile = LANES // S5_GROUP
    for q in range(u.shape[1] // LANES):
        for m in range(S5_ROW // LANES):
            bands = [u[(m * per_tile + jj) * rt:(m * per_tile + jj + 1) * rt,
                       q * LANES:(q + 1) * LANES] for jj in range(per_tile)]

            def put(gl, rows, q=q, m=m):
                ug_ref[q * per_tile + gl, 0, :, m * LANES:(m + 1) * LANES] = rows.astype(BF16)

            _lane_block_transpose(bands, put)


def _s5_rows_to_bands(yg_ref, ys_ref):
    rt = ys_ref.shape[0] // S5_CHUNK
    per_tile = LANES // S5_GROUP
    for q in range(ys_ref.shape[1] // LANES):
        for m in range(S5_ROW // LANES):
            rows = [yg_ref[q * per_tile + gl, 0, :, m * LANES:(m + 1) * LANES].astype(F32)
                    for gl in range(per_tile)]

            def put(jj, band, q=q, m=m):
                j = m * per_tile + jj
                ys_ref[j * rt:(j + 1) * rt, q * LANES:(q + 1) * LANES] = band

            _lane_block_transpose(rows, put)


def _in_odd_kernel(x_ref, mod_ref, g_ref, perm_ref, w_ref, ug_ref, gate_ref):
    d = x_ref.shape[-1]
    wu = w_ref.shape[1] // 2
    h = _norm_modulate(x_ref[0], g_ref[...], mod_ref[0], d).astype(BF16)
    hp = _dot(perm_ref[...], h).astype(BF16)
    gate_ref[0] = _dot(hp, w_ref[:, wu:]).astype(BF16)
    _bands_to_s5_rows(_dot(hp, w_ref[:, :wu]), ug_ref)


def _in_odd(x, mod, norm_g, w_in, tm):
    bsz, slen, d = x.shape
    wu = w_in.shape[1] // 2
    groups = wu // S5_GROUP
    rt = tm // S5_CHUNK
    tok = lambda wd: pl.BlockSpec((1, tm, wd), lambda b, i: (b, i, 0))
    return pl.pallas_call(
        _in_odd_kernel,
        grid=(bsz, slen // tm),
        in_specs=[tok(d), pl.BlockSpec((1, 1, 3 * d), lambda b, i: (b, 0, 0)),
                  _resident((1, d)), _resident((tm, tm)), _resident(w_in.shape)],
        out_specs=[pl.BlockSpec((groups, 1, rt, S5_ROW), lambda b, i: (0, b, i, 0)), tok(wu)],
        out_shape=[jax.ShapeDtypeStruct((groups, bsz, slen // S5_CHUNK, S5_ROW), BF16),
                   jax.ShapeDtypeStruct((bsz, slen, wu), BF16)],
        compiler_params=_cparams(("arbitrary", "arbitrary")),
        name="in_odd",
    )(x, mod, norm_g.reshape(1, d), _chunk_major_perm(tm), w_in.astype(BF16))


def _s5_weights(lam_re, lam_im, log_dt, b_re, b_im, c_re, c_im):
    hp = lax.Precision.HIGHEST
    L = S5_CHUNK
    groups = lam_re.shape[0]
    dt = jnp.exp(log_dt.astype(F32))[:, None]
    lam_re = lam_re.astype(F32)
    lam_im = lam_im.astype(F32)
    decay = jnp.exp(lam_re * dt)
    ang = lam_im * dt
    abar_re = decay * jnp.cos(ang)
    abar_im = decay * jnp.sin(ang)
    den = lam_re * lam_re + lam_im * lam_im
    num_re = abar_re - 1.0
    coef_re = (num_re * lam_re + abar_im * lam_im) / den
    coef_im = (abar_im * lam_re - num_re * lam_im) / den
    b_re = b_re.astype(F32)
    b_im = b_im.astype(F32)
    bbar_re = coef_re[..., None] * b_re - coef_im[..., None] * b_im
    bbar_im = coef_re[..., None] * b_im + coef_im[..., None] * b_re
    c_re = c_re.astype(F32)
    c_im = c_im.astype(F32)

    def power(n):
        n = n.astype(F32)[None, :, None]
        mag = jnp.exp(n * (lam_re * dt)[:, None, :])
        arg = n * ang[:, None, :]
        return mag * jnp.cos(arg), mag * jnp.sin(arg)

    pw_re, pw_im = power(jnp.arange(L + 1))
    ct_re = c_re.transpose(0, 2, 1)[:, :, None, :]
    ct_im = c_im.transpose(0, 2, 1)[:, :, None, :]
    pt_re = pw_re.transpose(0, 2, 1)[..., None]
    pt_im = pw_im.transpose(0, 2, 1)[..., None]
    ca_re = ct_re * pt_re - ct_im * pt_im
    ca_im = ct_re * pt_im + ct_im * pt_re
    bt_re = bbar_re.transpose(0, 2, 1)
    bt_im = bbar_im.transpose(0, 2, 1)
    strip = jnp.einsum(
        "gip,gpk->gik", jnp.concatenate([bt_re, -bt_im], axis=-1),
        jnp.concatenate([ca_re[:, :, :L], ca_im[:, :, :L]], axis=1).reshape(groups, 2 * S5_STATE, S5_ROW),
        precision=hp)
    padded = jnp.concatenate([jnp.zeros_like(strip), strip], axis=-1)
    toep = jnp.stack([padded[:, :, S5_ROW - S5_GROUP * ji:2 * S5_ROW - S5_GROUP * ji]
                      for ji in range(L)], axis=1).reshape(groups, S5_ROW, S5_ROW)

    rev_re = pw_re[:, L - 1::-1][:, :L, None, :]
    rev_im = pw_im[:, L - 1::-1][:, :L, None, :]
    bq_re, bq_im = bt_re[:, None], bt_im[:, None]
    bp_re = (rev_re * bq_re - rev_im * bq_im).reshape(groups, S5_ROW, S5_STATE)
    bp_im = (rev_re * bq_im + rev_im * bq_re).reshape(groups, S5_ROW, S5_STATE)
    zb = jnp.zeros_like(bp_re)
    bsel = jnp.concatenate([jnp.concatenate([bp_re, zb, bp_im, zb], -1),
                            jnp.concatenate([zb, bp_re, zb, bp_im], -1)], axis=1)

    cp_re = ca_re[:, :, 1:].reshape(groups, S5_STATE, S5_ROW)
    cp_im = -ca_im[:, :, 1:].reshape(groups, S5_STATE, S5_ROW)
    zc = jnp.zeros_like(cp_re)
    csel = jnp.concatenate([jnp.concatenate([cp_re, zc], -1), jnp.concatenate([zc, cp_re], -1),
                            jnp.concatenate([cp_im, zc], -1), jnp.concatenate([zc, cp_im], -1)], axis=1)

    def both_halves(re, im):
        return jnp.stack([jnp.concatenate([re, re], -1), jnp.concatenate([im, im], -1)], axis=2)

    lvl = both_halves(*power(L * jnp.array([1, 2, 4])))
    pw8 = both_halves(*power(L * jnp.arange(1, 9))).transpose(0, 2, 1, 3)
    return toep, bsel, csel, lvl, pw8


def _s5_kernel(u_ref, toep_ref, bsel_ref, csel_ref, lvl_ref, pw8_ref, d_ref, y_ref,
               hre_ref, him_ref):
    bsz, chunks, width = u_ref.shape[1], u_ref.shape[2], u_ref.shape[3]
    lanes = lvl_ref.shape[-1]
    groups8 = chunks // 8
    row = lax.broadcasted_iota(jnp.int32, (chunks, lanes), 0)
    sub = lax.broadcasted_iota(jnp.int32, (groups8, 8, lanes), 1)
    toep = toep_ref[0]
    pw_re, pw_im = pw8_ref[0, 0], pw8_ref[0, 1]
    for pair in range(bsz // 2):
        us = [u_ref[0, 2 * pair + s] for s in range(2)]
        x = _dot(jnp.concatenate(us, axis=1), bsel_ref[0])
        xr = jnp.where(row >= 1, pltpu.roll(x[:, :lanes], 1, 0), 0.0).reshape(groups8, 8, lanes)
        xi = jnp.where(row >= 1, pltpu.roll(x[:, lanes:], 1, 0), 0.0).reshape(groups8, 8, lanes)
        for k, d in enumerate((1, 2, 4)):
            cr, ci = lvl_ref[0, k, 0:1, :], lvl_ref[0, k, 1:2, :]
            pr = jnp.where(sub >= d, pltpu.roll(xr, d, 1), 0.0)
            pi = jnp.where(sub >= d, pltpu.roll(xi, d, 1), 0.0)
            xr, xi = xr + (cr * pr - ci * pi), xi + (cr * pi + ci * pr)
        car = jnp.zeros((1, lanes), F32)
        cai = jnp.zeros((1, lanes), F32)
        for g in range(groups8):
            hr = xr[g] + (pw_re * car - pw_im * cai)
            hi = xi[g] + (pw_re * cai + pw_im * car)
            hre_ref[pair, 8 * g:8 * g + 8, :] = hr
            him_ref[pair, 8 * g:8 * g + 8, :] = hi
            car, cai = hr[7:8, :], hi[7:8, :]
        h = jnp.concatenate([hre_ref[pair], him_ref[pair]], axis=1).astype(BF16)
        y = _dot(h, csel_ref[0])
        for s in range(2):
            ys = y[:, s * width:(s + 1) * width] + _dot(us[s], toep)
            ys = ys + d_ref[0] * us[s].astype(F32)
            y_ref[0, 2 * pair + s] = ys.astype(y_ref.dtype)


def _s5(ug, toep, bsel, csel, lvl, pw8, d_rows):
    groups, bsz, chunks, width = ug.shape
    assert bsz % 2 == 0 and chunks % 8 == 0
    gspec = lambda a: pl.BlockSpec((1,) + a.shape[1:], lambda g: (g,) + (0,) * (a.ndim - 1))
    tile = pl.BlockSpec((1, bsz, chunks, width), lambda g: (g, 0, 0, 0))
    state = pltpu.VMEM((bsz // 2, chunks, lvl.shape[-1]), F32)
    return pl.pallas_call(
        _s5_kernel,
        grid=(groups,),
        in_specs=[tile, gspec(toep), gspec(bsel), gspec(csel), gspec(lvl), gspec(pw8),
                  gspec(d_rows)],
        out_specs=tile,
        out_shape=jax.ShapeDtypeStruct(ug.shape, BF16),
        scratch_shapes=[state, state],
        compiler_params=_cparams(("arbitrary",)),
        name="s5",
    )(ug, toep.astype(BF16), bsel.astype(BF16), csel.astype(BF16), lvl, pw8, d_rows)


def _gelu_tanh(x):
    return 0.5 * x * (1.0 + jnp.tanh(math.sqrt(2.0 / math.pi) * (x + 0.044715 * (x * x * x))))


def _out_odd_kernel(x_ref, yg_ref, gate_ref, mod_ref, gw_ref, gb_ref, unperm_ref, w_ref,
                    o_ref, ys_ref):
    d = x_ref.shape[-1]
    _s5_rows_to_bands(yg_ref, ys_ref)
    y = _gelu_tanh(ys_ref[...])
    y = y * jax.nn.sigmoid(_dot(y.astype(BF16), gw_ref[...]) + gb_ref[...])
    y = (y * _silu(gate_ref[0].astype(F32))).astype(BF16)
    y = _dot(unperm_ref[...], y).astype(BF16)
    o_ref[0] = x_ref[0] + mod_ref[0][:, 2 * d:] * _dot(y, w_ref[...])


def _out_odd(x, yg, gate, mod, glu_w, glu_b, w_out, tm):
    bsz, slen, d = x.shape
    groups = yg.shape[0]
    w = gate.shape[-1]
    rt = tm // S5_CHUNK
    tok = lambda wd: pl.BlockSpec((1, tm, wd), lambda b, i: (b, i, 0))
    return pl.pallas_call(
        _out_odd_kernel,
        grid=(bsz, slen // tm),
        in_specs=[tok(d), pl.BlockSpec((groups, 1, rt, S5_ROW), lambda b, i: (0, b, i, 0)), tok(w),
                  pl.BlockSpec((1, 1, 3 * d), lambda b, i: (b, 0, 0)),
                  _resident(glu_w.shape), _resident((1, w)), _resident((tm, tm)),
                  _resident(w_out.shape)],
        out_specs=tok(d),
        out_shape=jax.ShapeDtypeStruct((bsz, slen, d), F32),
        scratch_shapes=[pltpu.VMEM((tm, w), F32)],
        compiler_params=_cparams(("arbitrary", "arbitrary")),
        name="out_odd",
    )(x, yg, gate, mod, glu_w.astype(BF16), glu_b.reshape(1, w), _chunk_major_perm(tm).T,
      w_out.astype(BF16))


def _even_layer(x, mod, norm_g, w_in, conv_w, conv_b, wr, br, wi, bi, lam, q_g, k_g, w_out):
    xa, ga, q, k, v, gb = _in_even(x, mod, norm_g, w_in, q_g, k_g, tm=512)
    ya = _lru(xa, ga, conv_w, conv_b, wr, br, wi, bi, lam, ts=256)
    yb = _sb_attn(q, k, v, gb, tq=256, heads_per_step=8)
    return _out_even(x, ya, yb, mod, w_out, tm=512)


def _odd_layer(x, mod, norm_g, w_in, lam_re, lam_im, log_dt, b_re, b_im, c_re, c_im, d_skip,
               glu_w, glu_b, w_out):
    tm = 512
    ug, gate = _in_odd(x, mod, norm_g, w_in, tm=tm)
    groups = ug.shape[0]
    weights = _s5_weights(lam_re, lam_im, log_dt, b_re, b_im, c_re, c_im)
    d_rows = jnp.tile(d_skip.astype(F32).reshape(groups, 1, S5_GROUP), (1, 1, S5_CHUNK))
    yg = _s5(ug, *weights, d_rows)
    return _out_odd(x, yg, gate, mod, glu_w, glu_b, w_out, tm=tm)


def kernel(x, c, norm_g, ada_w, ada_b, w_in_even, conv_w, conv_b, lru_wr, lru_br, lru_wi, lru_bi,
           lru_lambda, q_norm_g, k_norm_g, w_out_even, w_in_odd, s5_lambda_re, s5_lambda_im,
           s5_log_dt, s5_b_re, s5_b_im, s5_c_re, s5_c_im, s5_d, glu_w, glu_b, w_out_odd):
    depth = norm_g.shape[0]
    mods = _ada(c, ada_w, ada_b)
    for layer in range(depth):
        mod = mods[layer][:, None, :]
        j = layer // 2
        if layer % 2 == 0:
            x = _even_layer(x, mod, norm_g[layer], w_in_even[j], conv_w[j], conv_b[j], lru_wr[j],
                            lru_br[j], lru_wi[j], lru_bi[j], lru_lambda[j], q_norm_g[j],
                            k_norm_g[j], w_out_even[j])
        else:
            x = _odd_layer(x, mod, norm_g[layer], w_in_odd[j], s5_lambda_re[j], s5_lambda_im[j],
                           s5_log_dt[j], s5_b_re[j], s5_b_im[j], s5_c_re[j], s5_c_im[j], s5_d[j],
                           glu_w[j], glu_b[j], w_out_odd[j])
    return x
```

```python
import math

import jax
import jax.numpy as jnp
from jax import lax
from jax.experimental import pallas as pl
from jax.experimental.pallas import tpu as pltpu

F32 = jnp.float32
BF16 = jnp.bfloat16

EPS = 1e-6
LANES = 128
LRU_HEADS = 8
LRU_C = 8.0
CONV_WIDTH = 4
SB_HEAD_DIM = 128
S5_GROUP = 16
S5_STATE = 64
S5_CHUNK = 16
S5_ROW = S5_CHUNK * S5_GROUP

LOG2E = math.log2(math.e)
SB_DEAD_LOG2 = 110.0 * LOG2E

VMEM_LIMIT = 56 * 1024 * 1024


def _cparams(sem):
    return pltpu.CompilerParams(dimension_semantics=sem, vmem_limit_bytes=VMEM_LIMIT)


def _resident(shape):
    nd = len(shape)
    return pl.BlockSpec(shape, lambda *_: (0,) * nd, pipeline_mode=pl.Buffered(1))


def _silu(x):
    return x * jax.nn.sigmoid(x)


def _dot(a, b):
    return jnp.dot(a, b, preferred_element_type=F32)


def _ada_kernel(c_ref, w_ref, b_ref, o_ref):
    s = _silu(c_ref[...])
    o_ref[0] = jnp.dot(s, w_ref[0], preferred_element_type=F32,
                       precision=lax.Precision.HIGHEST) + b_ref[0]


def _ada(c, ada_w, ada_b):
    depth, d, d3 = ada_w.shape
    bsz = c.shape[0]
    rows = 8
    c_pad = jnp.zeros((rows, d), F32).at[:bsz].set(c)
    tn = 1024
    out = pl.pallas_call(
        _ada_kernel,
        grid=(depth, d3 // tn),
        in_specs=[
            pl.BlockSpec((rows, d), lambda l, n: (0, 0)),
            pl.BlockSpec((1, d, tn), lambda l, n: (l, 0, n)),
            pl.BlockSpec((1, 1, tn), lambda l, n: (l, 0, n)),
        ],
        out_specs=pl.BlockSpec((1, rows, tn), lambda l, n: (l, 0, n)),
        out_shape=jax.ShapeDtypeStruct((depth, rows, d3), F32),
        compiler_params=_cparams(("arbitrary", "arbitrary")),
        name="ada",
    )(c_pad, ada_w, ada_b.reshape(depth, 1, d3))
    return out[:, :bsz]


def _norm_modulate(x, g, mod, d):
    ms = jnp.mean(x * x, axis=-1, keepdims=True)
    y = x * lax.rsqrt(ms + EPS) * g
    return y * (1.0 + mod[:, d:2 * d]) + mod[:, :d]


def _head_rms(t, g_row, scale):
    outs = []
    for h in range(t.shape[1] // SB_HEAD_DIM):
        th = t[:, h * SB_HEAD_DIM:(h + 1) * SB_HEAD_DIM]
        ms = jnp.mean(th * th, axis=-1, keepdims=True)
        outs.append(th * lax.rsqrt(ms + EPS) * (g_row * scale))
    return outs


def _in_even_kernel(x_ref, mod_ref, g_ref, w_ref, qg_ref, kg_ref,
                    xa_ref, ga_ref, q_ref, k_ref, v_ref, gb_ref):
    d = x_ref.shape[-1]
    h = _norm_modulate(x_ref[0], g_ref[...], mod_ref[0], d).astype(BF16)
    xa_ref[0] = _dot(h, w_ref[:, 0 * d:1 * d])
    ga_ref[0] = _dot(h, w_ref[:, 1 * d:2 * d]).astype(BF16)
    q = _dot(h, w_ref[:, 2 * d:3 * d])
    for i, qh in enumerate(_head_rms(q, qg_ref[...], LOG2E * SB_HEAD_DIM ** -0.5)):
        q_ref[0, :, i * SB_HEAD_DIM:(i + 1) * SB_HEAD_DIM] = qh.astype(BF16)
    k = _dot(h, w_ref[:, 3 * d:4 * d])
    for i, kh in enumerate(_head_rms(k, kg_ref[...], 1.0)):
        k_ref[0, :, i * SB_HEAD_DIM:(i + 1) * SB_HEAD_DIM] = kh.astype(BF16)
    v_ref[0] = _dot(h, w_ref[:, 4 * d:5 * d]).astype(BF16)
    gb_ref[0] = _dot(h, w_ref[:, 5 * d:6 * d]).astype(BF16)


def _in_even(x, mod, norm_g, w_in, q_g, k_g, tm):
    bsz, slen, d = x.shape
    tok = pl.BlockSpec((1, tm, d), lambda b, i: (b, i, 0))
    shp = lambda dt: jax.ShapeDtypeStruct((bsz, slen, d), dt)
    return pl.pallas_call(
        _in_even_kernel,
        grid=(bsz, slen // tm),
        in_specs=[
            tok,
            pl.BlockSpec((1, 1, 3 * d), lambda b, i: (b, 0, 0)),
            _resident((1, d)),
            _resident(w_in.shape),
            _resident((1, SB_HEAD_DIM)),
            _resident((1, SB_HEAD_DIM)),
        ],
        out_specs=[tok] * 6,
        out_shape=[shp(F32), shp(BF16), shp(BF16), shp(BF16), shp(BF16), shp(BF16)],
        compiler_params=_cparams(("arbitrary", "arbitrary")),
        name="in_even",
    )(x, mod, norm_g.reshape(1, d), w_in.astype(BF16), q_g.reshape(1, -1), k_g.reshape(1, -1))


def _scan_rows8(a, b):
    sub = lax.broadcasted_iota(jnp.int32, a.shape, 1)
    for d in (1, 2, 4):
        valid = sub >= d
        a_prev = jnp.where(valid, pltpu.roll(a, d, 1), 1.0)
        b_prev = jnp.where(valid, pltpu.roll(b, d, 1), 0.0)
        b = a * b_prev + b
        a = a * a_prev
    return a, b


def _lru_kernel(xa_ref, ga_ref, cw_ref, cb_ref, wr_ref, br_ref, wi_ref, bi_ref, lam_ref,
                o_ref, tail_ref, hbuf_ref, h_ref):
    ts, w = xa_ref.shape[1], xa_ref.shape[2]
    blk = w // LRU_HEADS

    @pl.when(pl.program_id(1) == 0)
    def _():
        tail_ref[...] = jnp.zeros_like(tail_ref)
        h_ref[...] = jnp.zeros_like(h_ref)

    x = xa_ref[0]
    tail = tail_ref[...]
    sub = lax.broadcasted_iota(jnp.int32, (8, w), 0)
    xc = cb_ref[...] + cw_ref[CONV_WIDTH - 1:CONV_WIDTH, :] * x
    for back in range(1, CONV_WIDTH):
        xs = pltpu.roll(x, back, 0)
        top = jnp.where(sub < back, pltpu.roll(tail, back, 0), xs[:8])
        xs = jnp.concatenate([top, xs[8:]], axis=0)
        xc = xc + cw_ref[CONV_WIDTH - 1 - back:CONV_WIDTH - back, :] * xs
    tail_ref[...] = x[ts - 8:]

    xb = xc.astype(BF16)
    rs, is_ = [], []
    for hd in range(LRU_HEADS):
        xh = xb[:, hd * blk:(hd + 1) * blk]
        rs.append(_dot(xh, wr_ref[hd]))
        is_.append(_dot(xh, wi_ref[hd]))
    r = jax.nn.sigmoid(jnp.concatenate(rs, axis=1) + br_ref[...])
    ig = jax.nn.sigmoid(jnp.concatenate(is_, axis=1) + bi_ref[...])
    lam = lam_ref[...]
    log_sig_lam = jnp.minimum(lam, 0.0) - jnp.log(1.0 + jnp.exp(-jnp.abs(lam)))
    log_a = r * (LRU_C * log_sig_lam)
    a = jnp.exp(log_a)
    v = -jnp.tanh(log_a) * (a * a + 1.0)
    b = (v * lax.rsqrt(jnp.maximum(v, 1e-30))) * (ig * xc)

    groups = ts // 8
    a3, b3 = _scan_rows8(a.reshape(groups, 8, w), b.reshape(groups, 8, w))
    carry = h_ref[7:8, :]
    hg = None
    for gi in range(groups):
        hg = a3[gi] * carry + b3[gi]
        hbuf_ref[8 * gi:8 * gi + 8, :] = hg
        carry = hg[7:8, :]
    h_ref[...] = hg
    o_ref[0] = (hbuf_ref[...] * _silu(ga_ref[0].astype(F32))).astype(BF16)


def _lru(xa, ga, conv_w, conv_b, wr, br, wi, bi, lam, ts):
    bsz, slen, w = xa.shape
    tok = pl.BlockSpec((1, ts, w), lambda b, i: (b, i, 0))
    row = lambda v: v.reshape(1, w)
    return pl.pallas_call(
        _lru_kernel,
        grid=(bsz, slen // ts),
        in_specs=[tok, tok, _resident(conv_w.shape), _resident((1, w)),
                  _resident(wr.shape), _resident((1, w)), _resident(wi.shape), _resident((1, w)),
                  _resident((1, w))],
        out_specs=tok,
        out_shape=jax.ShapeDtypeStruct((bsz, slen, w), BF16),
        scratch_shapes=[pltpu.VMEM((8, w), F32), pltpu.VMEM((ts, w), F32),
                        pltpu.VMEM((8, w), F32)],
        compiler_params=_cparams(("arbitrary", "arbitrary")),
        name="lru",
    )(xa, ga, conv_w, row(conv_b), wr.astype(BF16), row(br), wi.astype(BF16), row(bi), row(lam))


def _sb_kernel(q_ref, k_ref, v_ref, gb_ref, o_ref, acc_ref, carry_ref):
    tq = q_ref.shape[1]
    dh = SB_HEAD_DIM
    heads = q_ref.shape[2] // dh
    row = lax.broadcasted_iota(jnp.int32, (tq, tq), 0)
    col = lax.broadcasted_iota(jnp.int32, (tq, tq), 1)
    after = jnp.where(row > col, 1.0, 0.0).astype(BF16)
    causal = col < row

    def scores(hd, start, diag):
        lanes = slice(hd * dh, (hd + 1) * dh)
        kblk = k_ref[0, pl.ds(start, tq), lanes]
        z = lax.dot_general(q_ref[0, :, lanes], kblk, (((1,), (1,)), ((), ())),
                            preferred_element_type=F32)
        mx = jnp.maximum(z, 0.0)
        mn = jnp.minimum(z, 0.0)
        l = jnp.log2(1.0 + jnp.exp2(mn - mx))
        sp = l + mx
        if diag:
            sp = jnp.where(causal, sp, 0.0)
        sp = sp.astype(BF16)
        later = _dot(sp, after)
        rowsum = jnp.broadcast_to(later[:, 0:1] + sp[:, 0:1].astype(F32), (tq, LANES))
        old = None if diag else carry_ref[hd]
        carry_ref[hd] = rowsum if diag else old + rowsum
        return later, mn - l, old

    def weights(hd, start, diag, later, log_beta, old):
        lanes = slice(hd * dh, (hd + 1) * dh)
        vblk = v_ref[0, pl.ds(start, tq), lanes]
        if diag:
            wgt = jnp.where(causal, jnp.exp2(log_beta - later), 0.0)
            acc_ref[hd] = _dot(wgt.astype(BF16), vblk)
        else:
            later = later + jnp.concatenate([old] * (tq // LANES), axis=1)
            acc_ref[hd] += _dot(jnp.exp2(log_beta - later).astype(BF16), vblk)

    def sweep(blocks):
        work = [(hd, start, diag) for start, diag in blocks for hd in range(heads)]
        st = [scores(*w) for w in work]
        m = carry_ref[0]
        for hd in range(1, heads):
            m = jnp.minimum(m, carry_ref[hd])
        alive = (jnp.min(m) < SB_DEAD_LOG2).astype(jnp.int32)
        for w, s in zip(work, st):
            weights(*w, *s)
        return alive

    q0 = pl.multiple_of(pl.program_id(2) * tq, tq)
    prev = pl.multiple_of(jnp.maximum(q0 - tq, 0), tq)
    alive = lax.cond(q0 > 0, lambda: sweep([(q0, True), (prev, False)]),
                     lambda: sweep([(q0, True)]))

    def cond(st):
        end, alive = st
        return jnp.logical_and(end > 0, alive > 0)

    def body(st):
        start = pl.multiple_of(st[0] - tq, tq)
        return start, sweep([(start, False)])

    lax.while_loop(cond, body, (prev, alive))
    for hd in range(heads):
        lanes = slice(hd * dh, (hd + 1) * dh)
        o_ref[0, :, lanes] = (acc_ref[hd] * _silu(gb_ref[0, :, lanes].astype(F32))).astype(BF16)


def _sb_attn(q, k, v, gb, tq, heads_per_step):
    bsz, slen, w = q.shape
    wd = heads_per_step * SB_HEAD_DIM
    qspec = pl.BlockSpec((1, tq, wd), lambda b, h, i: (b, i, h))
    kvspec = pl.BlockSpec((1, slen, wd), lambda b, h, i: (b, 0, h), pipeline_mode=pl.Buffered(1))
    return pl.pallas_call(
        _sb_kernel,
        grid=(bsz, w // wd, slen // tq),
        in_specs=[qspec, kvspec, kvspec, qspec],
        out_specs=qspec,
        out_shape=jax.ShapeDtypeStruct((bsz, slen, w), BF16),
        scratch_shapes=[pltpu.VMEM((heads_per_step, tq, SB_HEAD_DIM), F32),
                        pltpu.VMEM((heads_per_step, tq, LANES), F32)],
        compiler_params=_cparams(("arbitrary", "arbitrary", "arbitrary")),
        name="sb_attn",
    )(q, k, v, gb)


def _chunk_major_perm(tm):
    rt = tm // S5_CHUNK
    i = jnp.arange(tm)
    src = S5_CHUNK * (i % rt) + i // rt
    return (src[:, None] == jnp.arange(tm)[None, :]).astype(BF16)


def _lane_block_transpose(src, put):
    per_tile = LANES // S5_GROUP
    lane_blk = lax.broadcasted_iota(jnp.int32, src[0].shape, 1) // S5_GROUP
    rolled = []
    for d in range(per_tile):
        t = src[d]
        for a in range(1, per_tile):
            t = jnp.where(lane_blk == a, src[(a + d) % per_tile], t)
        rolled.append(pltpu.roll(t, S5_GROUP * d, 1) if d else t)
    for a in range(per_tile):
        out = rolled[(-a) % per_tile]
        for b in range(1, per_tile):
            out = jnp.where(lane_blk == b, rolled[(b - a) % per_tile], out)
        put(a, out)


def _bands_to_s5_rows(u, ug_ref):
    rt = u.shape[0] // S5_CHUNK
    per_tile = LANES // S5_GROUP
    for q in range(u.shape[1] // LANES):
        for m in range(S5_ROW // LANES):
            bands = [u[(m * per_tile + jj) * rt:(m * per_tile + jj + 1) * rt,
                       q * LANES:(q + 1) * LANES] for jj in range(per_tile)]

            def put(gl, rows, q=q, m=m):
                ug_ref[q * per_tile + gl, 0, :, m * LANES:(m + 1) * LANES] = rows.astype(BF16)

            _lane_block_transpose(bands, put)


def _s5_rows_to_bands(yg_ref, ys_ref):
    rt = ys_ref.shape[0] // S5_CHUNK
    per_tile = LANES // S5_GROUP
    for q in range(ys_ref.shape[1] // LANES):
        for m in range(S5_ROW // LANES):
            rows = [yg_ref[q * per_tile + gl, 0, :, m * LANES:(m + 1) * LANES].astype(F32)
                    for gl in range(per_tile)]

            def put(jj, band, q=q, m=m):
                j = m * per_tile + jj
                ys_ref[j * rt:(j + 1) * rt, q * LANES:(q + 1) * LANES] = band

            _lane_block_transpose(rows, put)


def _bridge_kernel(x_ref, ya_ref, yb_ref, mod0_ref, wo_ref, mod1_ref, g_ref, perm_ref, wi_ref,
                   x1_ref, ug_ref, gate_ref):
    d = x_ref.shape[-1]
    wa = ya_ref.shape[-1]
    wu = wi_ref.shape[1] // 2
    out = _dot(ya_ref[0], wo_ref[:wa, :]) + _dot(yb_ref[0], wo_ref[wa:, :])
    x1 = x_ref[0] + mod0_ref[0][:, 2 * d:] * out
    x1_ref[0] = x1
    h = _norm_modulate(x1, g_ref[...], mod1_ref[0], d).astype(BF16)
    hp = _dot(perm_ref[...], h).astype(BF16)
    gate_ref[0] = _dot(hp, wi_ref[:, wu:]).astype(BF16)
    _bands_to_s5_rows(_dot(hp, wi_ref[:, :wu]), ug_ref)


def _bridge(x, ya, yb, mod0, w_out, mod1, norm_g, w_in, tm):
    bsz, slen, d = x.shape
    wu = w_in.shape[1] // 2
    groups = wu // S5_GROUP
    rt = tm // S5_CHUNK
    tok = lambda wd: pl.BlockSpec((1, tm, wd), lambda b, i: (b, i, 0))
    modspec = pl.BlockSpec((1, 1, 3 * d), lambda b, i: (b, 0, 0))
    return pl.pallas_call(
        _bridge_kernel,
        grid=(bsz, slen // tm),
        in_specs=[tok(d), tok(ya.shape[-1]), tok(yb.shape[-1]), modspec, _resident(w_out.shape),
                  modspec, _resident((1, d)), _resident((tm, tm)), _resident(w_in.shape)],
        out_specs=[tok(d), pl.BlockSpec((groups, 1, rt, S5_ROW), lambda b, i: (0, b, i, 0)),
                   tok(wu)],
        out_shape=[jax.ShapeDtypeStruct((bsz, slen, d), F32),
                   jax.ShapeDtypeStruct((groups, bsz, slen // S5_CHUNK, S5_ROW), BF16),
                   jax.ShapeDtypeStruct((bsz, slen, wu), BF16)],
        compiler_params=_cparams(("arbitrary", "arbitrary")),
        name="bridge",
    )(x, ya, yb, mod0, w_out.astype(BF16), mod1, norm_g.reshape(1, d), _chunk_major_perm(tm),
      w_in.astype(BF16))


def _s5_weights(lam_re, lam_im, log_dt, b_re, b_im, c_re, c_im):
    hp = lax.Precision.HIGHEST
    L = S5_CHUNK
    groups = lam_re.shape[0]
    dt = jnp.exp(log_dt.astype(F32))[:, None]
    lam_re = lam_re.astype(F32)
    lam_im = lam_im.astype(F32)
    decay = jnp.exp(lam_re * dt)
    ang = lam_im * dt
    abar_re = decay * jnp.cos(ang)
    abar_im = decay * jnp.sin(ang)
    den = lam_re * lam_re + lam_im * lam_im
    num_re = abar_re - 1.0
    coef_re = (num_re * lam_re + abar_im * lam_im) / den
    coef_im = (abar_im * lam_re - num_re * lam_im) / den
    b_re = b_re.astype(F32)
    b_im = b_im.astype(F32)
    bbar_re = coef_re[..., None] * b_re - coef_im[..., None] * b_im
    bbar_im = coef_re[..., None] * b_im + coef_im[..., None] * b_re
    c_re = c_re.astype(F32)
    c_im = c_im.astype(F32)

    def power(n):
        n = n.astype(F32)[None, :, None]
        mag = jnp.exp(n * (lam_re * dt)[:, None, :])
        arg = n * ang[:, None, :]
        return mag * jnp.cos(arg), mag * jnp.sin(arg)

    all_re, all_im = power(jnp.concatenate([jnp.arange(L + 1), L * jnp.arange(2, 9)]))
    pw_re, pw_im = all_re[:, :L + 1], all_im[:, :L + 1]
    ct_re = c_re.transpose(0, 2, 1)[:, :, None, :]
    ct_im = c_im.transpose(0, 2, 1)[:, :, None, :]
    pt_re = pw_re.transpose(0, 2, 1)[..., None]
    pt_im = pw_im.transpose(0, 2, 1)[..., None]
    ca_re = ct_re * pt_re - ct_im * pt_im
    ca_im = ct_re * pt_im + ct_im * pt_re
    bt_re = bbar_re.transpose(0, 2, 1)
    bt_im = bbar_im.transpose(0, 2, 1)
    strip = jnp.einsum(
        "gip,gpk->gik", jnp.concatenate([bt_re, -bt_im], axis=-1),
        jnp.concatenate([ca_re[:, :, :L], ca_im[:, :, :L]], axis=1).reshape(groups, 2 * S5_STATE, S5_ROW),
        precision=hp)
    padded = jnp.concatenate([jnp.zeros_like(strip), strip], axis=-1)
    toep = jnp.stack([padded[:, :, S5_ROW - S5_GROUP * ji:2 * S5_ROW - S5_GROUP * ji]
                      for ji in range(L)], axis=1).reshape(groups, S5_ROW, S5_ROW)

    rev_re = pw_re[:, L - 1::-1][:, :L, None, :]
    rev_im = pw_im[:, L - 1::-1][:, :L, None, :]
    bq_re, bq_im = bt_re[:, None], bt_im[:, None]
    bp_re = (rev_re * bq_re - rev_im * bq_im).reshape(groups, S5_ROW, S5_STATE)
    bp_im = (rev_re * bq_im + rev_im * bq_re).reshape(groups, S5_ROW, S5_STATE)
    zb = jnp.zeros_like(bp_re)
    bsel = jnp.concatenate([jnp.concatenate([bp_re, zb, bp_im, zb], -1),
                            jnp.concatenate([zb, bp_re, zb, bp_im], -1)], axis=1)

    cp_re = ca_re[:, :, 1:].reshape(groups, S5_STATE, S5_ROW)
    cp_im = -ca_im[:, :, 1:].reshape(groups, S5_STATE, S5_ROW)
    zc = jnp.zeros_like(cp_re)
    csel = jnp.concatenate([jnp.concatenate([cp_re, zc], -1), jnp.concatenate([zc, cp_re], -1),
                            jnp.concatenate([cp_im, zc], -1), jnp.concatenate([zc, cp_im], -1)], axis=1)

    def both_halves(re, im):
        return jnp.stack([jnp.concatenate([re, re], -1), jnp.concatenate([im, im], -1)], axis=2)

    pw8 = both_halves(all_re[:, L:], all_im[:, L:])
    lvl = jnp.concatenate([pw8[:, 0:2], pw8[:, 3:4]], axis=1)
    pw8 = pw8.transpose(0, 2, 1, 3)
    return toep, bsel, csel, lvl, pw8


def _s5_kernel(u_ref, toep_ref, bsel_ref, csel_ref, lvl_ref, pw8_ref, d_ref, y_ref,
               hre_ref, him_ref):
    bsz, chunks, width = u_ref.shape[1], u_ref.shape[2], u_ref.shape[3]
    lanes = lvl_ref.shape[-1]
    groups8 = chunks // 8
    row = lax.broadcasted_iota(jnp.int32, (chunks, lanes), 0)
    sub = lax.broadcasted_iota(jnp.int32, (groups8, 8, lanes), 1)
    toep = toep_ref[0]
    pw_re, pw_im = pw8_ref[0, 0], pw8_ref[0, 1]
    for pair in range(bsz // 2):
        us = [u_ref[0, 2 * pair + s] for s in range(2)]
        x = _dot(jnp.concatenate(us, axis=1), bsel_ref[0])
        xr = jnp.where(row >= 1, pltpu.roll(x[:, :lanes], 1, 0), 0.0).reshape(groups8, 8, lanes)
        xi = jnp.where(row >= 1, pltpu.roll(x[:, lanes:], 1, 0), 0.0).reshape(groups8, 8, lanes)
        for k, d in enumerate((1, 2, 4)):
            cr, ci = lvl_ref[0, k, 0:1, :], lvl_ref[0, k, 1:2, :]
            pr = jnp.where(sub >= d, pltpu.roll(xr, d, 1), 0.0)
            pi = jnp.where(sub >= d, pltpu.roll(xi, d, 1), 0.0)
            xr, xi = xr + (cr * pr - ci * pi), xi + (cr * pi + ci * pr)
        car = jnp.zeros((1, lanes), F32)
        cai = jnp.zeros((1, lanes), F32)
        for g in range(groups8):
            hr = xr[g] + (pw_re * car - pw_im * cai)
            hi = xi[g] + (pw_re * cai + pw_im * car)
            hre_ref[pair, 8 * g:8 * g + 8, :] = hr
            him_ref[pair, 8 * g:8 * g + 8, :] = hi
            car, cai = hr[7:8, :], hi[7:8, :]
        h = jnp.concatenate([hre_ref[pair], him_ref[pair]], axis=1).astype(BF16)
        y = _dot(h, csel_ref[0])
        for s in range(2):
            ys = y[:, s * width:(s + 1) * width] + _dot(us[s], toep)
            ys = ys + d_ref[0] * us[s].astype(F32)
            y_ref[0, 2 * pair + s] = ys.astype(y_ref.dtype)


def _s5(ug, toep, bsel, csel, lvl, pw8, d_rows):
    groups, bsz, chunks, width = ug.shape
    assert bsz % 2 == 0 and chunks % 8 == 0
    gspec = lambda a: pl.BlockSpec((1,) + a.shape[1:], lambda g: (g,) + (0,) * (a.ndim - 1))
    tile = pl.BlockSpec((1, bsz, chunks, width), lambda g: (g, 0, 0, 0))
    state = pltpu.VMEM((bsz // 2, chunks, lvl.shape[-1]), F32)
    return pl.pallas_call(
        _s5_kernel,
        grid=(groups,),
        in_specs=[tile, gspec(toep), gspec(bsel), gspec(csel), gspec(lvl), gspec(pw8),
                  gspec(d_rows)],
        out_specs=tile,
        out_shape=jax.ShapeDtypeStruct(ug.shape, BF16),
        scratch_shapes=[state, state],
        compiler_params=_cparams(("arbitrary",)),
        name="s5",
    )(ug, toep.astype(BF16), bsel.astype(BF16), csel.astype(BF16), lvl, pw8, d_rows)


def _gelu_tanh(x):
    return 0.5 * x * (1.0 + jnp.tanh(math.sqrt(2.0 / math.pi) * (x + 0.044715 * (x * x * x))))


def _out_odd_kernel(x_ref, yg_ref, gate_ref, mod_ref, gw_ref, gb_ref, unperm_ref, w_ref,
                    o_ref, ys_ref):
    d = x_ref.shape[-1]
    _s5_rows_to_bands(yg_ref, ys_ref)
    y = _gelu_tanh(ys_ref[...])
    y = y * jax.nn.sigmoid(_dot(y.astype(BF16), gw_ref[...]) + gb_ref[...])
    y = (y * _silu(gate_ref[0].astype(F32))).astype(BF16)
    y = _dot(unperm_ref[...], y).astype(BF16)
    o_ref[0] = x_ref[0] + mod_ref[0][:, 2 * d:] * _dot(y, w_ref[...])


def _out_odd(x, yg, gate, mod, glu_w, glu_b, w_out, tm):
    bsz, slen, d = x.shape
    groups = yg.shape[0]
    w = gate.shape[-1]
    rt = tm // S5_CHUNK
    tok = lambda wd: pl.BlockSpec((1, tm, wd), lambda b, i: (b, i, 0))
    return pl.pallas_call(
        _out_odd_kernel,
        grid=(bsz, slen // tm),
        in_specs=[tok(d), pl.BlockSpec((groups, 1, rt, S5_ROW), lambda b, i: (0, b, i, 0)), tok(w),
                  pl.BlockSpec((1, 1, 3 * d), lambda b, i: (b, 0, 0)),
                  _resident(glu_w.shape), _resident((1, w)), _resident((tm, tm)),
                  _resident(w_out.shape)],
        out_specs=tok(d),
        out_shape=jax.ShapeDtypeStruct((bsz, slen, d), F32),
        scratch_shapes=[pltpu.VMEM((tm, w), F32)],
        compiler_params=_cparams(("arbitrary", "arbitrary")),
        name="out_odd",
    )(x, yg, gate, mod, glu_w.astype(BF16), glu_b.reshape(1, w), _chunk_major_perm(tm).T,
      w_out.astype(BF16))


def _layer_pair(x, mod0, mod1, norm_g0, norm_g1, w_in_even, conv_w, conv_b, wr, br, wi, bi, lam,
                q_g, k_g, w_out_even, w_in_odd, lam_re, lam_im, log_dt, b_re, b_im, c_re, c_im,
                d_skip, glu_w, glu_b, w_out_odd):
    tm = 512
    xa, ga, q, k, v, gb = _in_even(x, mod0, norm_g0, w_in_even, q_g, k_g, tm=512)
    ya = _lru(xa, ga, conv_w, conv_b, wr, br, wi, bi, lam, ts=512)
    yb = _sb_attn(q, k, v, gb, tq=256, heads_per_step=8)
    x, ug, gate = _bridge(x, ya, yb, mod0, w_out_even, mod1, norm_g1, w_in_odd, tm=tm)
    groups = ug.shape[0]
    weights = _s5_weights(lam_re, lam_im, log_dt, b_re, b_im, c_re, c_im)
    d_rows = jnp.tile(d_skip.astype(F32).reshape(groups, 1, S5_GROUP), (1, 1, S5_CHUNK))
    yg = _s5(ug, *weights, d_rows)
    return _out_odd(x, yg, gate, mod1, glu_w, glu_b, w_out_odd, tm=tm)


def kernel(x, c, norm_g, ada_w, ada_b, w_in_even, conv_w, conv_b, lru_wr, lru_br, lru_wi, lru_bi,
           lru_lambda, q_norm_g, k_norm_g, w_out_even, w_in_odd, s5_lambda_re, s5_lambda_im,
           s5_log_dt, s5_b_re, s5_b_im, s5_c_re, s5_c_im, s5_d, glu_w, glu_b, w_out_odd):
    depth = norm_g.shape[0]
    assert depth % 2 == 0, "layers are fused in (even, odd) pairs"
    mods = _ada(c, ada_w, ada_b)[:, :, None, :]
    for j in range(depth // 2):
        x = _layer_pair(x, mods[2 * j], mods[2 * j + 1], norm_g[2 * j], norm_g[2 * j + 1],
                        w_in_even[j], conv_w[j], conv_b[j], lru_wr[j], lru_br[j], lru_wi[j],
                        lru_bi[j], lru_lambda[j], q_norm_g[j], k_norm_g[j], w_out_even[j],
                        w_in_odd[j], s5_lambda_re[j], s5_lambda_im[j], s5_log_dt[j], s5_b_re[j],
                        s5_b_im[j], s5_c_re[j], s5_c_im[j], s5_d[j], glu_w[j], glu_b[j],
                        w_out_odd[j])
    return x
```

```python
import math

import jax
import jax.numpy as jnp
from jax import lax
from jax.experimental import pallas as pl
from jax.experimental.pallas import tpu as pltpu

F32 = jnp.float32
BF16 = jnp.bfloat16

EPS = 1e-6
LANES = 128
LRU_HEADS = 8
LRU_C = 8.0
CONV_WIDTH = 4
SB_HEAD_DIM = 128
S5_GROUP = 16
S5_STATE = 64
S5_CHUNK = 16
S5_ROW = S5_CHUNK * S5_GROUP

LOG2E = math.log2(math.e)
SB_DEAD_LOG2 = 110.0 * LOG2E

VMEM_LIMIT = 56 * 1024 * 1024


def _cparams(sem):
    return pltpu.CompilerParams(dimension_semantics=sem, vmem_limit_bytes=VMEM_LIMIT)


def _resident(shape):
    nd = len(shape)
    return pl.BlockSpec(shape, lambda *_: (0,) * nd, pipeline_mode=pl.Buffered(1))


def _silu(x):
    return x * jax.nn.sigmoid(x)


def _dot(a, b):
    return jnp.dot(a, b, preferred_element_type=F32)


def _ada_kernel(c_ref, w_ref, b_ref, o_ref):
    s = _silu(c_ref[...])
    o_ref[0] = jnp.dot(s, w_ref[0], preferred_element_type=F32,
                       precision=lax.Precision.HIGHEST) + b_ref[0]


def _ada(c, ada_w, ada_b):
    depth, d, d3 = ada_w.shape
    bsz = c.shape[0]
    rows = 8
    c_pad = jnp.zeros((rows, d), F32).at[:bsz].set(c)
    tn = d3
    out = pl.pallas_call(
        _ada_kernel,
        grid=(depth, d3 // tn),
        in_specs=[
            pl.BlockSpec((rows, d), lambda l, n: (0, 0)),
            pl.BlockSpec((1, d, tn), lambda l, n: (l, 0, n)),
            pl.BlockSpec((1, 1, tn), lambda l, n: (l, 0, n)),
        ],
        out_specs=pl.BlockSpec((1, rows, tn), lambda l, n: (l, 0, n)),
        out_shape=jax.ShapeDtypeStruct((depth, rows, d3), F32),
        compiler_params=_cparams(("arbitrary", "arbitrary")),
        name="ada",
    )(c_pad, ada_w, ada_b.reshape(depth, 1, d3))
    return out[:, :bsz]


def _norm_modulate(x, g, mod, d):
    ms = jnp.mean(x * x, axis=-1, keepdims=True)
    y = x * lax.rsqrt(ms + EPS) * g
    return y * (1.0 + mod[:, d:2 * d]) + mod[:, :d]


def _head_rms(t, g_row, scale):
    outs = []
    for h in range(t.shape[1] // SB_HEAD_DIM):
        th = t[:, h * SB_HEAD_DIM:(h + 1) * SB_HEAD_DIM]
        ms = jnp.mean(th * th, axis=-1, keepdims=True)
        outs.append(th * lax.rsqrt(ms + EPS) * (g_row * scale))
    return outs


def _in_even_kernel(x_ref, mod_ref, g_ref, w_ref, qg_ref, kg_ref,
                    xa_ref, ga_ref, q_ref, k_ref, v_ref, gb_ref):
    d = x_ref.shape[-1]
    h = _norm_modulate(x_ref[0], g_ref[...], mod_ref[0], d).astype(BF16)
    xa_ref[0] = _dot(h, w_ref[:, 0 * d:1 * d])
    ga_ref[0] = _dot(h, w_ref[:, 1 * d:2 * d]).astype(BF16)
    q = _dot(h, w_ref[:, 2 * d:3 * d])
    for i, qh in enumerate(_head_rms(q, qg_ref[...], LOG2E * SB_HEAD_DIM ** -0.5)):
        q_ref[0, :, i * SB_HEAD_DIM:(i + 1) * SB_HEAD_DIM] = qh.astype(BF16)
    k = _dot(h, w_ref[:, 3 * d:4 * d])
    for i, kh in enumerate(_head_rms(k, kg_ref[...], 1.0)):
        k_ref[0, :, i * SB_HEAD_DIM:(i + 1) * SB_HEAD_DIM] = kh.astype(BF16)
    v_ref[0] = _dot(h, w_ref[:, 4 * d:5 * d]).astype(BF16)
    gb_ref[0] = _dot(h, w_ref[:, 5 * d:6 * d]).astype(BF16)


def _in_even(x, mod, norm_g, w_in, q_g, k_g, tm):
    bsz, slen, d = x.shape
    tok = pl.BlockSpec((1, tm, d), lambda b, i: (b, i, 0))
    shp = lambda dt: jax.ShapeDtypeStruct((bsz, slen, d), dt)
    return pl.pallas_call(
        _in_even_kernel,
        grid=(bsz, slen // tm),
        in_specs=[
            tok,
            pl.BlockSpec((1, 1, 3 * d), lambda b, i: (b, 0, 0)),
            _resident((1, d)),
            _resident(w_in.shape),
            _resident((1, SB_HEAD_DIM)),
            _resident((1, SB_HEAD_DIM)),
        ],
        out_specs=[tok] * 6,
        out_shape=[shp(F32), shp(BF16), shp(BF16), shp(BF16), shp(BF16), shp(BF16)],
        compiler_params=_cparams(("arbitrary", "arbitrary")),
        name="in_even",
    )(x, mod, norm_g.reshape(1, d), w_in.astype(BF16), q_g.reshape(1, -1), k_g.reshape(1, -1))


def _scan_rows8(a, b):
    sub = lax.broadcasted_iota(jnp.int32, a.shape, 1)
    for d in (1, 2, 4):
        valid = sub >= d
        a_prev = jnp.where(valid, pltpu.roll(a, d, 1), 1.0)
        b_prev = jnp.where(valid, pltpu.roll(b, d, 1), 0.0)
        b = a * b_prev + b
        a = a * a_prev
    return a, b


def _lru_kernel(xa_ref, ga_ref, cw_ref, cb_ref, wr_ref, br_ref, wi_ref, bi_ref, lam_ref,
                o_ref, tail_ref, hbuf_ref, h_ref):
    ts, w = xa_ref.shape[1], xa_ref.shape[2]
    blk = w // LRU_HEADS

    @pl.when(pl.program_id(1) == 0)
    def _():
        tail_ref[...] = jnp.zeros_like(tail_ref)
        h_ref[...] = jnp.zeros_like(h_ref)

    x = xa_ref[0]
    tail = tail_ref[...]
    sub = lax.broadcasted_iota(jnp.int32, (8, w), 0)
    xc = cb_ref[...] + cw_ref[CONV_WIDTH - 1:CONV_WIDTH, :] * x
    for back in range(1, CONV_WIDTH):
        xs = pltpu.roll(x, back, 0)
        top = jnp.where(sub < back, pltpu.roll(tail, back, 0), xs[:8])
        xs = jnp.concatenate([top, xs[8:]], axis=0)
        xc = xc + cw_ref[CONV_WIDTH - 1 - back:CONV_WIDTH - back, :] * xs
    tail_ref[...] = x[ts - 8:]

    xb = xc.astype(BF16)
    rs, is_ = [], []
    for hd in range(LRU_HEADS):
        xh = xb[:, hd * blk:(hd + 1) * blk]
        rs.append(_dot(xh, wr_ref[hd]))
        is_.append(_dot(xh, wi_ref[hd]))
    r = jax.nn.sigmoid(jnp.concatenate(rs, axis=1) + br_ref[...])
    ig = jax.nn.sigmoid(jnp.concatenate(is_, axis=1) + bi_ref[...])
    lam = lam_ref[...]
    log_sig_lam = jnp.minimum(lam, 0.0) - jnp.log(1.0 + jnp.exp(-jnp.abs(lam)))
    log_a = r * (LRU_C * log_sig_lam)
    a = jnp.exp(log_a)
    v = -jnp.tanh(log_a) * (a * a + 1.0)
    b = (v * lax.rsqrt(jnp.maximum(v, 1e-30))) * (ig * xc)

    groups = ts // 8
    a3, b3 = _scan_rows8(a.reshape(groups, 8, w), b.reshape(groups, 8, w))
    carry = h_ref[7:8, :]
    hg = None
    for gi in range(groups):
        hg = a3[gi] * carry + b3[gi]
        hbuf_ref[8 * gi:8 * gi + 8, :] = hg
        carry = hg[7:8, :]
    h_ref[...] = hg
    o_ref[0] = (hbuf_ref[...] * _silu(ga_ref[0].astype(F32))).astype(BF16)


def _lru(xa, ga, conv_w, conv_b, wr, br, wi, bi, lam, ts):
    bsz, slen, w = xa.shape
    tok = pl.BlockSpec((1, ts, w), lambda b, i: (b, i, 0))
    row = lambda v: v.reshape(1, w)
    return pl.pallas_call(
        _lru_kernel,
        grid=(bsz, slen // ts),
        in_specs=[tok, tok, _resident(conv_w.shape), _resident((1, w)),
                  _resident(wr.shape), _resident((1, w)), _resident(wi.shape), _resident((1, w)),
                  _resident((1, w))],
        out_specs=tok,
        out_shape=jax.ShapeDtypeStruct((bsz, slen, w), BF16),
        scratch_shapes=[pltpu.VMEM((8, w), F32), pltpu.VMEM((ts, w), F32),
                        pltpu.VMEM((8, w), F32)],
        compiler_params=_cparams(("arbitrary", "arbitrary")),
        name="lru",
    )(xa, ga, conv_w, row(conv_b), wr.astype(BF16), row(br), wi.astype(BF16), row(bi), row(lam))


def _sb_kernel(q_ref, k_ref, v_ref, gb_ref, o_ref, acc_ref, carry_ref):
    tq = q_ref.shape[1]
    dh = SB_HEAD_DIM
    heads = q_ref.shape[2] // dh
    row = lax.broadcasted_iota(jnp.int32, (tq, tq), 0)
    col = lax.broadcasted_iota(jnp.int32, (tq, tq), 1)
    after = jnp.where(row > col, 1.0, 0.0).astype(BF16)
    causal = col < row

    def scores(hd, start, diag):
        lanes = slice(hd * dh, (hd + 1) * dh)
        kblk = k_ref[0, pl.ds(start, tq), lanes]
        z = lax.dot_general(q_ref[0, :, lanes], kblk, (((1,), (1,)), ((), ())),
                            preferred_element_type=F32)
        mx = jnp.maximum(z, 0.0)
        mn = jnp.minimum(z, 0.0)
        l = jnp.log2(1.0 + jnp.exp2(mn - mx))
        sp = l + mx
        if diag:
            sp = jnp.where(causal, sp, 0.0)
        sp = sp.astype(BF16)
        later = _dot(sp, after)
        rowsum = jnp.broadcast_to(later[:, 0:1] + sp[:, 0:1].astype(F32), (tq, LANES))
        old = None if diag else carry_ref[hd]
        carry_ref[hd] = rowsum if diag else old + rowsum
        return later, mn - l, old

    def weights(hd, start, diag, later, log_beta, old):
        lanes = slice(hd * dh, (hd + 1) * dh)
        vblk = v_ref[0, pl.ds(start, tq), lanes]
        if diag:
            wgt = jnp.where(causal, jnp.exp2(log_beta - later), 0.0)
            acc_ref[hd] = _dot(wgt.astype(BF16), vblk)
        else:
            later = later + jnp.concatenate([old] * (tq // LANES), axis=1)
            acc_ref[hd] += _dot(jnp.exp2(log_beta - later).astype(BF16), vblk)

    def sweep(blocks):
        work = [(hd, start, diag) for start, diag in blocks for hd in range(heads)]
        st = [scores(*w) for w in work]
        m = carry_ref[0]
        for hd in range(1, heads):
            m = jnp.minimum(m, carry_ref[hd])
        alive = (jnp.min(m) < SB_DEAD_LOG2).astype(jnp.int32)
        for w, s in zip(work, st):
            weights(*w, *s)
        return alive

    q0 = pl.multiple_of(pl.program_id(2) * tq, tq)
    prev = pl.multiple_of(jnp.maximum(q0 - tq, 0), tq)
    alive = lax.cond(q0 > 0, lambda: sweep([(q0, True), (prev, False)]),
                     lambda: sweep([(q0, True)]))

    def cond(st):
        end, alive = st
        return jnp.logical_and(end > 0, alive > 0)

    def body(st):
        start = pl.multiple_of(st[0] - tq, tq)
        return start, sweep([(start, False)])

    lax.while_loop(cond, body, (prev, alive))
    for hd in range(heads):
        lanes = slice(hd * dh, (hd + 1) * dh)
        o_ref[0, :, lanes] = (acc_ref[hd] * _silu(gb_ref[0, :, lanes].astype(F32))).astype(BF16)


def _sb_attn(q, k, v, gb, tq, heads_per_step):
    bsz, slen, w = q.shape
    wd = heads_per_step * SB_HEAD_DIM
    qspec = pl.BlockSpec((1, tq, wd), lambda b, h, i: (b, i, h))
    kvspec = pl.BlockSpec((1, slen, wd), lambda b, h, i: (b, 0, h), pipeline_mode=pl.Buffered(1))
    return pl.pallas_call(
        _sb_kernel,
        grid=(bsz, w // wd, slen // tq),
        in_specs=[qspec, kvspec, kvspec, qspec],
        out_specs=qspec,
        out_shape=jax.ShapeDtypeStruct((bsz, slen, w), BF16),
        scratch_shapes=[pltpu.VMEM((heads_per_step, tq, SB_HEAD_DIM), F32),
                        pltpu.VMEM((heads_per_step, tq, LANES), F32)],
        compiler_params=_cparams(("arbitrary", "arbitrary", "arbitrary")),
        name="sb_attn",
    )(q, k, v, gb)


def _chunk_major_perm(tm):
    rt = tm // S5_CHUNK
    i = jnp.arange(tm)
    src = S5_CHUNK * (i % rt) + i // rt
    return (src[:, None] == jnp.arange(tm)[None, :]).astype(BF16)


def _lane_block_transpose(src, put):
    per_tile = LANES // S5_GROUP
    lane_blk = lax.broadcasted_iota(jnp.int32, src[0].shape, 1) // S5_GROUP
    rolled = []
    for d in range(per_tile):
        t = src[d]
        for a in range(1, per_tile):
            t = jnp.where(lane_blk == a, src[(a + d) % per_tile], t)
        rolled.append(pltpu.roll(t, S5_GROUP * d, 1) if d else t)
    for a in range(per_tile):
        out = rolled[(-a) % per_tile]
        for b in range(1, per_tile):
            out = jnp.where(lane_blk == b, rolled[(b - a) % per_tile], out)
        put(a, out)


def _bands_to_s5_rows(u, ug_ref):
    rt = u.shape[0] // S5_CHUNK
    per_tile = LANES // S5_GROUP
    for q in range(u.shape[1] // LANES):
        for m in range(S5_ROW // LANES):
            bands = [u[(m * per_tile + jj) * rt:(m * per_tile + jj + 1) * rt,
                       q * LANES:(q + 1) * LANES] for jj in range(per_tile)]

            def put(gl, rows, q=q, m=m):
                ug_ref[q * per_tile + gl, 0, :, m * LANES:(m + 1) * LANES] = rows.astype(BF16)

            _lane_block_transpose(bands, put)


def _s5_rows_to_bands(yg_ref, ys_ref):
    rt = ys_ref.shape[0] // S5_CHUNK
    per_tile = LANES // S5_GROUP
    for q in range(ys_ref.shape[1] // LANES):
        for m in range(S5_ROW // LANES):
            rows = [yg_ref[q * per_tile + gl, 0, :, m * LANES:(m + 1) * LANES].astype(F32)
                    for gl in range(per_tile)]

            def put(jj, band, q=q, m=m):
                j = m * per_tile + jj
                ys_ref[j * rt:(j + 1) * rt, q * LANES:(q + 1) * LANES] = band

            _lane_block_transpose(rows, put)


def _bridge_kernel(x_ref, ya_ref, yb_ref, mod0_ref, wo_ref, mod1_ref, g_ref, perm_ref, wi_ref,
                   x1_ref, ug_ref, gate_ref):
    d = x_ref.shape[-1]
    wa = ya_ref.shape[-1]
    wu = wi_ref.shape[1] // 2
    out = _dot(ya_ref[0], wo_ref[:wa, :]) + _dot(yb_ref[0], wo_ref[wa:, :])
    x1 = x_ref[0] + mod0_ref[0][:, 2 * d:] * out
    x1_ref[0] = x1
    h = _norm_modulate(x1, g_ref[...], mod1_ref[0], d).astype(BF16)
    hp = _dot(perm_ref[...], h).astype(BF16)
    gate_ref[0] = _dot(hp, wi_ref[:, wu:]).astype(BF16)
    _bands_to_s5_rows(_dot(hp, wi_ref[:, :wu]), ug_ref)


def _bridge(x, ya, yb, mod0, w_out, mod1, norm_g, w_in, tm):
    bsz, slen, d = x.shape
    wu = w_in.shape[1] // 2
    groups = wu // S5_GROUP
    rt = tm // S5_CHUNK
    tok = lambda wd: pl.BlockSpec((1, tm, wd), lambda b, i: (b, i, 0))
    modspec = pl.BlockSpec((1, 1, 3 * d), lambda b, i: (b, 0, 0))
    return pl.pallas_call(
        _bridge_kernel,
        grid=(bsz, slen // tm),
        in_specs=[tok(d), tok(ya.shape[-1]), tok(yb.shape[-1]), modspec, _resident(w_out.shape),
                  modspec, _resident((1, d)), _resident((tm, tm)), _resident(w_in.shape)],
        out_specs=[tok(d), pl.BlockSpec((groups, 1, rt, S5_ROW), lambda b, i: (0, b, i, 0)),
                   tok(wu)],
        out_shape=[jax.ShapeDtypeStruct((bsz, slen, d), F32),
                   jax.ShapeDtypeStruct((groups, bsz, slen // S5_CHUNK, S5_ROW), BF16),
                   jax.ShapeDtypeStruct((bsz, slen, wu), BF16)],
        compiler_params=_cparams(("arbitrary", "arbitrary")),
        name="bridge",
    )(x, ya, yb, mod0, w_out.astype(BF16), mod1, norm_g.reshape(1, d), _chunk_major_perm(tm),
      w_in.astype(BF16))


def _s5_weights(lam_re, lam_im, log_dt, b_re, b_im, c_re, c_im):
    hp = lax.Precision.HIGHEST
    L = S5_CHUNK
    groups = lam_re.shape[0]
    dt = jnp.exp(log_dt.astype(F32))[:, None]
    lam_re = lam_re.astype(F32)
    lam_im = lam_im.astype(F32)
    decay = jnp.exp(lam_re * dt)
    ang = lam_im * dt
    abar_re = decay * jnp.cos(ang)
    abar_im = decay * jnp.sin(ang)
    den = lam_re * lam_re + lam_im * lam_im
    num_re = abar_re - 1.0
    coef_re = (num_re * lam_re + abar_im * lam_im) / den
    coef_im = (abar_im * lam_re - num_re * lam_im) / den
    b_re = b_re.astype(F32)
    b_im = b_im.astype(F32)
    bbar_re = coef_re[..., None] * b_re - coef_im[..., None] * b_im
    bbar_im = coef_re[..., None] * b_im + coef_im[..., None] * b_re
    c_re = c_re.astype(F32)
    c_im = c_im.astype(F32)

    def power(n):
        n = n.astype(F32)[None, :, None]
        mag = jnp.exp(n * (lam_re * dt)[:, None, :])
        arg = n * ang[:, None, :]
        return mag * jnp.cos(arg), mag * jnp.sin(arg)

    all_re, all_im = power(jnp.concatenate([jnp.arange(L + 1), L * jnp.arange(2, 9)]))
    pw_re, pw_im = all_re[:, :L + 1], all_im[:, :L + 1]
    ct_re = c_re.transpose(0, 2, 1)[:, :, None, :]
    ct_im = c_im.transpose(0, 2, 1)[:, :, None, :]
    pt_re = pw_re.transpose(0, 2, 1)[..., None]
    pt_im = pw_im.transpose(0, 2, 1)[..., None]
    ca_re = ct_re * pt_re - ct_im * pt_im
    ca_im = ct_re * pt_im + ct_im * pt_re
    bt_re = bbar_re.transpose(0, 2, 1)
    bt_im = bbar_im.transpose(0, 2, 1)
    strip = jnp.einsum(
        "gip,gpk->gik", jnp.concatenate([bt_re, -bt_im], axis=-1),
        jnp.concatenate([ca_re[:, :, :L], ca_im[:, :, :L]], axis=1).reshape(groups, 2 * S5_STATE, S5_ROW),
        precision=hp)
    padded = jnp.concatenate([jnp.zeros_like(strip), strip], axis=-1)
    toep = jnp.stack([padded[:, :, S5_ROW - S5_GROUP * ji:2 * S5_ROW - S5_GROUP * ji]
                      for ji in range(L)], axis=1).reshape(groups, S5_ROW, S5_ROW)

    rev_re = pw_re[:, L - 1::-1][:, :L, None, :]
    rev_im = pw_im[:, L - 1::-1][:, :L, None, :]
    bq_re, bq_im = bt_re[:, None], bt_im[:, None]
    bp_re = (rev_re * bq_re - rev_im * bq_im).reshape(groups, S5_ROW, S5_STATE)
    bp_im = (rev_re * bq_im + rev_im * bq_re).reshape(groups, S5_ROW, S5_STATE)
    zb = jnp.zeros_like(bp_re)
    bsel = jnp.concatenate([jnp.concatenate([bp_re, zb, bp_im, zb], -1),
                            jnp.concatenate([zb, bp_re, zb, bp_im], -1)], axis=1)

    cp_re = ca_re[:, :, 1:].reshape(groups, S5_STATE, S5_ROW)
    cp_im = -ca_im[:, :, 1:].reshape(groups, S5_STATE, S5_ROW)
    zc = jnp.zeros_like(cp_re)
    csel = jnp.concatenate([jnp.concatenate([cp_re, zc], -1), jnp.concatenate([zc, cp_re], -1),
                            jnp.concatenate([cp_im, zc], -1), jnp.concatenate([zc, cp_im], -1)], axis=1)

    def both_halves(re, im):
        return jnp.stack([jnp.concatenate([re, re], -1), jnp.concatenate([im, im], -1)], axis=2)

    pw8 = both_halves(all_re[:, L:], all_im[:, L:])
    lvl = jnp.concatenate([pw8[:, 0:2], pw8[:, 3:4]], axis=1)
    pw8 = pw8.transpose(0, 2, 1, 3)
    return toep, bsel, csel, lvl, pw8


def _s5_kernel(u_ref, toep_ref, bsel_ref, csel_ref, lvl_ref, pw8_ref, d_ref, y_ref,
               hre_ref, him_ref):
    bsz, chunks, width = u_ref.shape[1], u_ref.shape[2], u_ref.shape[3]
    lanes = lvl_ref.shape[-1]
    groups8 = chunks // 8
    row = lax.broadcasted_iota(jnp.int32, (chunks, lanes), 0)
    sub = lax.broadcasted_iota(jnp.int32, (groups8, 8, lanes), 1)
    toep = toep_ref[0]
    pw_re, pw_im = pw8_ref[0, 0], pw8_ref[0, 1]
    for pair in range(bsz // 2):
        us = [u_ref[0, 2 * pair + s] for s in range(2)]
        x = _dot(jnp.concatenate(us, axis=1), bsel_ref[0])
        xr = jnp.where(row >= 1, pltpu.roll(x[:, :lanes], 1, 0), 0.0).reshape(groups8, 8, lanes)
        xi = jnp.where(row >= 1, pltpu.roll(x[:, lanes:], 1, 0), 0.0).reshape(groups8, 8, lanes)
        for k, d in enumerate((1, 2, 4)):
            cr, ci = lvl_ref[0, k, 0:1, :], lvl_ref[0, k, 1:2, :]
            pr = jnp.where(sub >= d, pltpu.roll(xr, d, 1), 0.0)
            pi = jnp.where(sub >= d, pltpu.roll(xi, d, 1), 0.0)
            xr, xi = xr + (cr * pr - ci * pi), xi + (cr * pi + ci * pr)
        car = jnp.zeros((1, lanes), F32)
        cai = jnp.zeros((1, lanes), F32)
        for g in range(groups8):
            hr = xr[g] + (pw_re * car - pw_im * cai)
            hi = xi[g] + (pw_re * cai + pw_im * car)
            hre_ref[pair, 8 * g:8 * g + 8, :] = hr
            him_ref[pair, 8 * g:8 * g + 8, :] = hi
            car, cai = hr[7:8, :], hi[7:8, :]
        h = jnp.concatenate([hre_ref[pair], him_ref[pair]], axis=1).astype(BF16)
        y = _dot(h, csel_ref[0])
        for s in range(2):
            ys = y[:, s * width:(s + 1) * width] + _dot(us[s], toep)
            ys = ys + d_ref[0] * us[s].astype(F32)
            y_ref[0, 2 * pair + s] = ys.astype(y_ref.dtype)


def _s5(ug, toep, bsel, csel, lvl, pw8, d_rows):
    groups, bsz, chunks, width = ug.shape
    assert bsz % 2 == 0 and chunks % 8 == 0
    gspec = lambda a: pl.BlockSpec((1,) + a.shape[1:], lambda g: (g,) + (0,) * (a.ndim - 1))
    tile = pl.BlockSpec((1, bsz, chunks, width), lambda g: (g, 0, 0, 0))
    state = pltpu.VMEM((bsz // 2, chunks, lvl.shape[-1]), F32)
    return pl.pallas_call(
        _s5_kernel,
        grid=(groups,),
        in_specs=[tile, gspec(toep), gspec(bsel), gspec(csel), gspec(lvl), gspec(pw8),
                  gspec(d_rows)],
        out_specs=tile,
        out_shape=jax.ShapeDtypeStruct(ug.shape, BF16),
        scratch_shapes=[state, state],
        compiler_params=_cparams(("arbitrary",)),
        name="s5",
    )(ug, toep.astype(BF16), bsel.astype(BF16), csel.astype(BF16), lvl, pw8, d_rows)


def _gelu_tanh(x):
    return 0.5 * x * (1.0 + jnp.tanh(math.sqrt(2.0 / math.pi) * (x + 0.044715 * (x * x * x))))


def _out_odd_kernel(x_ref, yg_ref, gate_ref, mod_ref, gw_ref, gb_ref, unperm_ref, w_ref,
                    o_ref, ys_ref):
    d = x_ref.shape[-1]
    _s5_rows_to_bands(yg_ref, ys_ref)
    y = _gelu_tanh(ys_ref[...])
    y = y * jax.nn.sigmoid(_dot(y.astype(BF16), gw_ref[...]) + gb_ref[...])
    y = (y * _silu(gate_ref[0].astype(F32))).astype(BF16)
    y = _dot(unperm_ref[...], y).astype(BF16)
    o_ref[0] = x_ref[0] + mod_ref[0][:, 2 * d:] * _dot(y, w_ref[...])


def _out_odd(x, yg, gate, mod, glu_w, glu_b, w_out, tm):
    bsz, slen, d = x.shape
    groups = yg.shape[0]
    w = gate.shape[-1]
    rt = tm // S5_CHUNK
    tok = lambda wd: pl.BlockSpec((1, tm, wd), lambda b, i: (b, i, 0))
    return pl.pallas_call(
        _out_odd_kernel,
        grid=(bsz, slen // tm),
        in_specs=[tok(d), pl.BlockSpec((groups, 1, rt, S5_ROW), lambda b, i: (0, b, i, 0)), tok(w),
                  pl.BlockSpec((1, 1, 3 * d), lambda b, i: (b, 0, 0)),
                  _resident(glu_w.shape), _resident((1, w)), _resident((tm, tm)),
                  _resident(w_out.shape)],
        out_specs=tok(d),
        out_shape=jax.ShapeDtypeStruct((bsz, slen, d), F32),
        scratch_shapes=[pltpu.VMEM((tm, w), F32)],
        compiler_params=_cparams(("arbitrary", "arbitrary")),
        name="out_odd",
    )(x, yg, gate, mod, glu_w.astype(BF16), glu_b.reshape(1, w), _chunk_major_perm(tm).T,
      w_out.astype(BF16))


def _layer_pair(x, mod0, mod1, norm_g0, norm_g1, w_in_even, conv_w, conv_b, wr, br, wi, bi, lam,
                q_g, k_g, w_out_even, w_in_odd, lam_re, lam_im, log_dt, b_re, b_im, c_re, c_im,
                d_skip, glu_w, glu_b, w_out_odd):
    tm = 512
    xa, ga, q, k, v, gb = _in_even(x, mod0, norm_g0, w_in_even, q_g, k_g, tm=1024)
    ya = _lru(xa, ga, conv_w, conv_b, wr, br, wi, bi, lam, ts=1024)
    yb = _sb_attn(q, k, v, gb, tq=256, heads_per_step=8)
    x, ug, gate = _bridge(x, ya, yb, mod0, w_out_even, mod1, norm_g1, w_in_odd, tm=tm)
    groups = ug.shape[0]
    weights = _s5_weights(lam_re, lam_im, log_dt, b_re, b_im, c_re, c_im)
    d_rows = jnp.tile(d_skip.astype(F32).reshape(groups, 1, S5_GROUP), (1, 1, S5_CHUNK))
    yg = _s5(ug, *weights, d_rows)
    return _out_odd(x, yg, gate, mod1, glu_w, glu_b, w_out_odd, tm=tm)


def kernel(x, c, norm_g, ada_w, ada_b, w_in_even, conv_w, conv_b, lru_wr, lru_br, lru_wi, lru_bi,
           lru_lambda, q_norm_g, k_norm_g, w_out_even, w_in_odd, s5_lambda_re, s5_lambda_im,
           s5_log_dt, s5_b_re, s5_b_im, s5_c_re, s5_c_im, s5_d, glu_w, glu_b, w_out_odd):
    depth = norm_g.shape[0]
    assert depth % 2 == 0, "layers are fused in (even, odd) pairs"
    mods = _ada(c, ada_w, ada_b)[:, :, None, :]
    for j in range(depth // 2):
        x = _layer_pair(x, mods[2 * j], mods[2 * j + 1], norm_g[2 * j], norm_g[2 * j + 1],
                        w_in_even[j], conv_w[j], conv_b[j], lru_wr[j], lru_br[j], lru_wi[j],
                        lru_bi[j], lru_lambda[j], q_norm_g[j], k_norm_g[j], w_out_even[j],
                        w_in_odd[j], s5_lambda_re[j], s5_lambda_im[j], s5_log_dt[j], s5_b_re[j],
                        s5_b_im[j], s5_c_re[j], s5_c_im[j], s5_d[j], glu_w[j], glu_b[j],
                        w_out_odd[j])
    return x
```

```python
import math

import jax
import jax.numpy as jnp
from jax import lax
from jax.experimental import pallas as pl
from jax.experimental.pallas import tpu as pltpu

F32 = jnp.float32
BF16 = jnp.bfloat16

EPS = 1e-6
LANES = 128
LRU_HEADS = 8
LRU_C = 8.0
CONV_WIDTH = 4
SB_HEAD_DIM = 128
S5_GROUP = 16
S5_STATE = 64
S5_CHUNK = 16
S5_ROW = S5_CHUNK * S5_GROUP

LOG2E = math.log2(math.e)
SB_DEAD_LOG2 = 110.0 * LOG2E

VMEM_LIMIT = 56 * 1024 * 1024


def _cparams(sem):
    return pltpu.CompilerParams(dimension_semantics=sem, vmem_limit_bytes=VMEM_LIMIT)


def _resident(shape):
    nd = len(shape)
    return pl.BlockSpec(shape, lambda *_: (0,) * nd, pipeline_mode=pl.Buffered(1))


def _silu(x):
    return x * jax.nn.sigmoid(x)


def _dot(a, b):
    return jnp.dot(a, b, preferred_element_type=F32)


def _ada_kernel(c_ref, w_ref, b_ref, o_ref):
    s = _silu(c_ref[...])
    o_ref[0] = jnp.dot(s, w_ref[0], preferred_element_type=F32,
                       precision=lax.Precision.HIGHEST) + b_ref[0]


def _ada(c, ada_w, ada_b):
    depth, d, d3 = ada_w.shape
    bsz = c.shape[0]
    rows = 8
    c_pad = jnp.zeros((rows, d), F32).at[:bsz].set(c)
    tn = 1024
    out = pl.pallas_call(
        _ada_kernel,
        grid=(depth, d3 // tn),
        in_specs=[
            pl.BlockSpec((rows, d), lambda l, n: (0, 0)),
            pl.BlockSpec((1, d, tn), lambda l, n: (l, 0, n)),
            pl.BlockSpec((1, 1, tn), lambda l, n: (l, 0, n)),
        ],
        out_specs=pl.BlockSpec((1, rows, tn), lambda l, n: (l, 0, n)),
        out_shape=jax.ShapeDtypeStruct((depth, rows, d3), F32),
        compiler_params=_cparams(("arbitrary", "arbitrary")),
        name="ada",
    )(c_pad, ada_w, ada_b.reshape(depth, 1, d3))
    return out[:, :bsz]


def _norm_modulate(x, g, mod, d):
    ms = jnp.mean(x * x, axis=-1, keepdims=True)
    y = x * lax.rsqrt(ms + EPS) * g
    return y * (1.0 + mod[:, d:2 * d]) + mod[:, :d]


def _head_rms(t, g_row, scale):
    outs = []
    for h in range(t.shape[1] // SB_HEAD_DIM):
        th = t[:, h * SB_HEAD_DIM:(h + 1) * SB_HEAD_DIM]
        ms = jnp.mean(th * th, axis=-1, keepdims=True)
        outs.append(th * lax.rsqrt(ms + EPS) * (g_row * scale))
    return outs


def _in_even_kernel(x_ref, mod_ref, g_ref, w_ref, qg_ref, kg_ref,
                    xa_ref, ga_ref, q_ref, k_ref, v_ref, gb_ref):
    d = x_ref.shape[-1]
    h = _norm_modulate(x_ref[0], g_ref[...], mod_ref[0], d).astype(BF16)
    xa_ref[0] = _dot(h, w_ref[:, 0 * d:1 * d])
    ga_ref[0] = _dot(h, w_ref[:, 1 * d:2 * d]).astype(BF16)
    q = _dot(h, w_ref[:, 2 * d:3 * d])
    for i, qh in enumerate(_head_rms(q, qg_ref[...], LOG2E * SB_HEAD_DIM ** -0.5)):
        q_ref[0, :, i * SB_HEAD_DIM:(i + 1) * SB_HEAD_DIM] = qh.astype(BF16)
    k = _dot(h, w_ref[:, 3 * d:4 * d])
    for i, kh in enumerate(_head_rms(k, kg_ref[...], 1.0)):
        k_ref[0, :, i * SB_HEAD_DIM:(i + 1) * SB_HEAD_DIM] = kh.astype(BF16)
    v_ref[0] = _dot(h, w_ref[:, 4 * d:5 * d]).astype(BF16)
    gb_ref[0] = _dot(h, w_ref[:, 5 * d:6 * d]).astype(BF16)


def _in_even(x, mod, norm_g, w_in, q_g, k_g, tm):
    bsz, slen, d = x.shape
    tok = pl.BlockSpec((1, tm, d), lambda b, i: (b, i, 0))
    shp = lambda dt: jax.ShapeDtypeStruct((bsz, slen, d), dt)
    return pl.pallas_call(
        _in_even_kernel,
        grid=(bsz, slen // tm),
        in_specs=[
            tok,
            pl.BlockSpec((1, 1, 3 * d), lambda b, i: (b, 0, 0)),
            _resident((1, d)),
            _resident(w_in.shape),
            _resident((1, SB_HEAD_DIM)),
            _resident((1, SB_HEAD_DIM)),
        ],
        out_specs=[tok] * 6,
        out_shape=[shp(F32), shp(BF16), shp(BF16), shp(BF16), shp(BF16), shp(BF16)],
        compiler_params=_cparams(("arbitrary", "arbitrary")),
        name="in_even",
    )(x, mod, norm_g.reshape(1, d), w_in.astype(BF16), q_g.reshape(1, -1), k_g.reshape(1, -1))


def _scan_rows8(a, b):
    sub = lax.broadcasted_iota(jnp.int32, a.shape, 1)
    for d in (1, 2, 4):
        valid = sub >= d
        a_prev = jnp.where(valid, pltpu.roll(a, d, 1), 1.0)
        b_prev = jnp.where(valid, pltpu.roll(b, d, 1), 0.0)
        b = a * b_prev + b
        a = a * a_prev
    return a, b


def _lru_kernel(xa_ref, ga_ref, cw_ref, cb_ref, wr_ref, br_ref, wi_ref, bi_ref, lam_ref,
                o_ref, tail_ref, hbuf_ref, h_ref):
    ts, w = xa_ref.shape[1], xa_ref.shape[2]
    blk = w // LRU_HEADS

    @pl.when(pl.program_id(1) == 0)
    def _():
        tail_ref[...] = jnp.zeros_like(tail_ref)
        h_ref[...] = jnp.zeros_like(h_ref)

    x = xa_ref[0]
    tail = tail_ref[...]
    sub = lax.broadcasted_iota(jnp.int32, (8, w), 0)
    xc = cb_ref[...] + cw_ref[CONV_WIDTH - 1:CONV_WIDTH, :] * x
    for back in range(1, CONV_WIDTH):
        xs = pltpu.roll(x, back, 0)
        top = jnp.where(sub < back, pltpu.roll(tail, back, 0), xs[:8])
        xs = jnp.concatenate([top, xs[8:]], axis=0)
        xc = xc + cw_ref[CONV_WIDTH - 1 - back:CONV_WIDTH - back, :] * xs
    tail_ref[...] = x[ts - 8:]

    xb = xc.astype(BF16)
    rs, is_ = [], []
    for hd in range(LRU_HEADS):
        xh = xb[:, hd * blk:(hd + 1) * blk]
        rs.append(_dot(xh, wr_ref[hd]))
        is_.append(_dot(xh, wi_ref[hd]))
    r = jax.nn.sigmoid(jnp.concatenate(rs, axis=1) + br_ref[...])
    ig = jax.nn.sigmoid(jnp.concatenate(is_, axis=1) + bi_ref[...])
    lam = lam_ref[...]
    log_sig_lam = jnp.minimum(lam, 0.0) - jnp.log(1.0 + jnp.exp(-jnp.abs(lam)))
    log_a = r * (LRU_C * log_sig_lam)
    a = jnp.exp(log_a)
    v = -jnp.tanh(log_a) * (a * a + 1.0)
    b = (v * lax.rsqrt(jnp.maximum(v, 1e-30))) * (ig * xc)

    groups = ts // 8
    a3, b3 = _scan_rows8(a.reshape(groups, 8, w), b.reshape(groups, 8, w))
    carry = h_ref[7:8, :]
    hg = None
    for gi in range(groups):
        hg = a3[gi] * carry + b3[gi]
        hbuf_ref[8 * gi:8 * gi + 8, :] = hg
        carry = hg[7:8, :]
    h_ref[...] = hg
    o_ref[0] = (hbuf_ref[...] * _silu(ga_ref[0].astype(F32))).astype(BF16)


def _lru(xa, ga, conv_w, conv_b, wr, br, wi, bi, lam, ts):
    bsz, slen, w = xa.shape
    tok = pl.BlockSpec((1, ts, w), lambda b, i: (b, i, 0))
    row = lambda v: v.reshape(1, w)
    return pl.pallas_call(
        _lru_kernel,
        grid=(bsz, slen // ts),
        in_specs=[tok, tok, _resident(conv_w.shape), _resident((1, w)),
                  _resident(wr.shape), _resident((1, w)), _resident(wi.shape), _resident((1, w)),
                  _resident((1, w))],
        out_specs=tok,
        out_shape=jax.ShapeDtypeStruct((bsz, slen, w), BF16),
        scratch_shapes=[pltpu.VMEM((8, w), F32), pltpu.VMEM((ts, w), F32),
                        pltpu.VMEM((8, w), F32)],
        compiler_params=_cparams(("arbitrary", "arbitrary")),
        name="lru",
    )(xa, ga, conv_w, row(conv_b), wr.astype(BF16), row(br), wi.astype(BF16), row(bi), row(lam))


def _kv_chunk_copy(hbm, vmem, sem, batch, which, chunk, rows, lanes):
    src = hbm.at[batch, pl.ds(chunk * rows, rows), lanes]
    return pltpu.make_async_copy(src, vmem.at[pl.ds(chunk * rows, rows), :], sem.at[which, chunk])


def _sb_kernel(q_ref, k_hbm, v_hbm, gb_ref, o_ref, k_ref, v_ref, kv_sem, acc_ref, carry_ref):
    tq = q_ref.shape[1]
    dh = SB_HEAD_DIM
    heads = q_ref.shape[2] // dh

    tile = pl.program_id(2)
    group_lanes = pl.ds(pl.multiple_of(pl.program_id(1) * heads * dh, LANES), heads * dh)
    copies = [(k_hbm, k_ref, 0), (v_hbm, v_ref, 1)]

    @pl.when(tile == 0)
    def _():
        for chunk in range(k_ref.shape[0] // tq):
            for hbm, vmem, which in copies:
                _kv_chunk_copy(hbm, vmem, kv_sem, pl.program_id(0), which, chunk, tq,
                               group_lanes).start()

    for hbm, vmem, which in copies:
        _kv_chunk_copy(hbm, vmem, kv_sem, pl.program_id(0), which, tile, tq, group_lanes).wait()

    row = lax.broadcasted_iota(jnp.int32, (tq, tq), 0)
    col = lax.broadcasted_iota(jnp.int32, (tq, tq), 1)
    after = jnp.where(row > col, 1.0, 0.0).astype(BF16)
    causal = col < row

    def scores(hd, start, diag):
        lanes = slice(hd * dh, (hd + 1) * dh)
        kblk = k_ref[pl.ds(start, tq), lanes]
        z = lax.dot_general(q_ref[0, :, lanes], kblk, (((1,), (1,)), ((), ())),
                            preferred_element_type=F32)
        mx = jnp.maximum(z, 0.0)
        mn = jnp.minimum(z, 0.0)
        l = jnp.log2(1.0 + jnp.exp2(mn - mx))
        sp = l + mx
        if diag:
            sp = jnp.where(causal, sp, 0.0)
        sp = sp.astype(BF16)
        later = _dot(sp, after)
        rowsum = jnp.broadcast_to(later[:, 0:1] + sp[:, 0:1].astype(F32), (tq, LANES))
        old = None if diag else carry_ref[hd]
        carry_ref[hd] = rowsum if diag else old + rowsum
        return later, mn - l, old

    def weights(hd, start, diag, later, log_beta, old):
        lanes = slice(hd * dh, (hd + 1) * dh)
        vblk = v_ref[pl.ds(start, tq), lanes]
        if diag:
            wgt = jnp.where(causal, jnp.exp2(log_beta - later), 0.0)
            acc_ref[hd] = _dot(wgt.astype(BF16), vblk)
        else:
            later = later + jnp.concatenate([old] * (tq // LANES), axis=1)
            acc_ref[hd] += _dot(jnp.exp2(log_beta - later).astype(BF16), vblk)

    def sweep(blocks):
        work = [(hd, start, diag) for start, diag in blocks for hd in range(heads)]
        st = [scores(*w) for w in work]
        m = carry_ref[0]
        for hd in range(1, heads):
            m = jnp.minimum(m, carry_ref[hd])
        alive = (jnp.min(m) < SB_DEAD_LOG2).astype(jnp.int32)
        for w, s in zip(work, st):
            weights(*w, *s)
        return alive

    q0 = pl.multiple_of(pl.program_id(2) * tq, tq)
    prev = pl.multiple_of(jnp.maximum(q0 - tq, 0), tq)
    alive = lax.cond(q0 > 0, lambda: sweep([(q0, True), (prev, False)]),
                     lambda: sweep([(q0, True)]))

    def cond(st):
        end, alive = st
        return jnp.logical_and(end > 0, alive > 0)

    def body(st):
        start = pl.multiple_of(st[0] - tq, tq)
        return start, sweep([(start, False)])

    lax.while_loop(cond, body, (prev, alive))
    for hd in range(heads):
        lanes = slice(hd * dh, (hd + 1) * dh)
        o_ref[0, :, lanes] = (acc_ref[hd] * _silu(gb_ref[0, :, lanes].astype(F32))).astype(BF16)


def _sb_attn(q, k, v, gb, tq, heads_per_step):
    bsz, slen, w = q.shape
    wd = heads_per_step * SB_HEAD_DIM
    qspec = pl.BlockSpec((1, tq, wd), lambda b, h, i: (b, i, h))
    kvspec = pl.BlockSpec(memory_space=pl.ANY)
    kv_vmem = pltpu.VMEM((slen, wd), BF16)
    return pl.pallas_call(
        _sb_kernel,
        grid=(bsz, w // wd, slen // tq),
        in_specs=[qspec, kvspec, kvspec, qspec],
        out_specs=qspec,
        out_shape=jax.ShapeDtypeStruct((bsz, slen, w), BF16),
        scratch_shapes=[kv_vmem, kv_vmem, pltpu.SemaphoreType.DMA((2, slen // tq)),
                        pltpu.VMEM((heads_per_step, tq, SB_HEAD_DIM), F32),
                        pltpu.VMEM((heads_per_step, tq, LANES), F32)],
        compiler_params=_cparams(("arbitrary", "arbitrary", "arbitrary")),
        name="sb_attn",
    )(q, k, v, gb)


def _chunk_major_perm(tm):
    rt = tm // S5_CHUNK
    i = jnp.arange(tm)
    src = S5_CHUNK * (i % rt) + i // rt
    return (src[:, None] == jnp.arange(tm)[None, :]).astype(BF16)


def _lane_block_transpose(src, put):
    per_tile = LANES // S5_GROUP
    lane_blk = lax.broadcasted_iota(jnp.int32, src[0].shape, 1) // S5_GROUP
    rolled = []
    for d in range(per_tile):
        t = src[d]
        for a in range(1, per_tile):
            t = jnp.where(lane_blk == a, src[(a + d) % per_tile], t)
        rolled.append(pltpu.roll(t, S5_GROUP * d, 1) if d else t)
    for a in range(per_tile):
        out = rolled[(-a) % per_tile]
        for b in range(1, per_tile):
            out = jnp.where(lane_blk == b, rolled[(b - a) % per_tile], out)
        put(a, out)


def _bands_to_s5_rows(u, ug_ref):
    rt = u.shape[0] // S5_CHUNK
    per_tile = LANES // S5_GROUP
    for q in range(u.shape[1] // LANES):
        for m in range(S5_ROW // LANES):
            bands = [u[(m * per_tile + jj) * rt:(m * per_tile + jj + 1) * rt,
                       q * LANES:(q + 1) * LANES] for jj in range(per_tile)]

            def put(gl, rows, q=q, m=m):
                ug_ref[q * per_tile + gl, 0, :, m * LANES:(m + 1) * LANES] = rows.astype(BF16)

            _lane_block_transpose(bands, put)


def _s5_rows_to_bands(yg_ref, ys_ref):
    rt = ys_ref.shape[0] // S5_CHUNK
    per_tile = LANES // S5_GROUP
    for q in range(ys_ref.shape[1] // LANES):
        for m in range(S5_ROW // LANES):
            rows = [yg_ref[q * per_tile + gl, 0, :, m * LANES:(m + 1) * LANES].astype(F32)
                    for gl in range(per_tile)]

            def put(jj, band, q=q, m=m):
                j = m * per_tile + jj
                ys_ref[j * rt:(j + 1) * rt, q * LANES:(q + 1) * LANES] = band

            _lane_block_transpose(rows, put)


def _bridge_kernel(x_ref, ya_ref, yb_ref, mod0_ref, wo_ref, mod1_ref, g_ref, perm_ref, wi_ref,
                   x1_ref, ug_ref, gate_ref):
    d = x_ref.shape[-1]
    wa = ya_ref.shape[-1]
    wu = wi_ref.shape[1] // 2
    out = _dot(ya_ref[0], wo_ref[:wa, :]) + _dot(yb_ref[0], wo_ref[wa:, :])
    x1 = x_ref[0] + mod0_ref[0][:, 2 * d:] * out
    x1_ref[0] = x1
    h = _norm_modulate(x1, g_ref[...], mod1_ref[0], d).astype(BF16)
    hp = _dot(perm_ref[...], h).astype(BF16)
    gate_ref[0] = _dot(hp, wi_ref[:, wu:]).astype(BF16)
    _bands_to_s5_rows(_dot(hp, wi_ref[:, :wu]), ug_ref)


def _bridge(x, ya, yb, mod0, w_out, mod1, norm_g, w_in, tm):
    bsz, slen, d = x.shape
    wu = w_in.shape[1] // 2
    groups = wu // S5_GROUP
    rt = tm // S5_CHUNK
    tok = lambda wd: pl.BlockSpec((1, tm, wd), lambda b, i: (b, i, 0))
    modspec = pl.BlockSpec((1, 1, 3 * d), lambda b, i: (b, 0, 0))
    return pl.pallas_call(
        _bridge_kernel,
        grid=(bsz, slen // tm),
        in_specs=[tok(d), tok(ya.shape[-1]), tok(yb.shape[-1]), modspec, _resident(w_out.shape),
                  modspec, _resident((1, d)), _resident((tm, tm)), _resident(w_in.shape)],
        out_specs=[tok(d), pl.BlockSpec((groups, 1, rt, S5_ROW), lambda b, i: (0, b, i, 0)),
                   tok(wu)],
        out_shape=[jax.ShapeDtypeStruct((bsz, slen, d), F32),
                   jax.ShapeDtypeStruct((groups, bsz, slen // S5_CHUNK, S5_ROW), BF16),
                   jax.ShapeDtypeStruct((bsz, slen, wu), BF16)],
        compiler_params=_cparams(("arbitrary", "arbitrary")),
        name="bridge",
    )(x, ya, yb, mod0, w_out.astype(BF16), mod1, norm_g.reshape(1, d), _chunk_major_perm(tm),
      w_in.astype(BF16))


def _s5_weights(lam_re, lam_im, log_dt, b_re, b_im, c_re, c_im):
    hp = lax.Precision.HIGHEST
    L = S5_CHUNK
    groups = lam_re.shape[0]
    dt = jnp.exp(log_dt.astype(F32))[:, None]
    lam_re = lam_re.astype(F32)
    lam_im = lam_im.astype(F32)
    decay = jnp.exp(lam_re * dt)
    ang = lam_im * dt
    abar_re = decay * jnp.cos(ang)
    abar_im = decay * jnp.sin(ang)
    den = lam_re * lam_re + lam_im * lam_im
    num_re = abar_re - 1.0
    coef_re = (num_re * lam_re + abar_im * lam_im) / den
    coef_im = (abar_im * lam_re - num_re * lam_im) / den
    b_re = b_re.astype(F32)
    b_im = b_im.astype(F32)
    bbar_re = coef_re[..., None] * b_re - coef_im[..., None] * b_im
    bbar_im = coef_re[..., None] * b_im + coef_im[..., None] * b_re
    c_re = c_re.astype(F32)
    c_im = c_im.astype(F32)

    def power(n):
        n = n.astype(F32)[None, :, None]
        mag = jnp.exp(n * (lam_re * dt)[:, None, :])
        arg = n * ang[:, None, :]
        return mag * jnp.cos(arg), mag * jnp.sin(arg)

    all_re, all_im = power(jnp.concatenate([jnp.arange(L + 1), L * jnp.arange(2, 9)]))
    pw_re, pw_im = all_re[:, :L + 1], all_im[:, :L + 1]
    ct_re = c_re.transpose(0, 2, 1)[:, :, None, :]
    ct_im = c_im.transpose(0, 2, 1)[:, :, None, :]
    pt_re = pw_re.transpose(0, 2, 1)[..., None]
    pt_im = pw_im.transpose(0, 2, 1)[..., None]
    ca_re = ct_re * pt_re - ct_im * pt_im
    ca_im = ct_re * pt_im + ct_im * pt_re
    bt_re = bbar_re.transpose(0, 2, 1)
    bt_im = bbar_im.transpose(0, 2, 1)
    strip = jnp.einsum(
        "gip,gpk->gik", jnp.concatenate([bt_re, -bt_im], axis=-1),
        jnp.concatenate([ca_re[:, :, :L], ca_im[:, :, :L]], axis=1).reshape(groups, 2 * S5_STATE, S5_ROW),
        precision=hp)
    padded = jnp.concatenate([jnp.zeros_like(strip), strip], axis=-1)
    toep = jnp.stack([padded[:, :, S5_ROW - S5_GROUP * ji:2 * S5_ROW - S5_GROUP * ji]
                      for ji in range(L)], axis=1).reshape(groups, S5_ROW, S5_ROW)

    rev_re = pw_re[:, L - 1::-1][:, :L, None, :]
    rev_im = pw_im[:, L - 1::-1][:, :L, None, :]
    bq_re, bq_im = bt_re[:, None], bt_im[:, None]
    bp_re = (rev_re * bq_re - rev_im * bq_im).reshape(groups, S5_ROW, S5_STATE)
    bp_im = (rev_re * bq_im + rev_im * bq_re).reshape(groups, S5_ROW, S5_STATE)
    zb = jnp.zeros_like(bp_re)
    bsel = jnp.concatenate([jnp.concatenate([bp_re, zb, bp_im, zb], -1),
                            jnp.concatenate([zb, bp_re, zb, bp_im], -1)], axis=1)

    cp_re = ca_re[:, :, 1:].reshape(groups, S5_STATE, S5_ROW)
    cp_im = -ca_im[:, :, 1:].reshape(groups, S5_STATE, S5_ROW)
    zc = jnp.zeros_like(cp_re)
    csel = jnp.concatenate([jnp.concatenate([cp_re, zc], -1), jnp.concatenate([zc, cp_re], -1),
                            jnp.concatenate([cp_im, zc], -1), jnp.concatenate([zc, cp_im], -1)], axis=1)

    def both_halves(re, im):
        return jnp.stack([jnp.concatenate([re, re], -1), jnp.concatenate([im, im], -1)], axis=2)

    pw8 = both_halves(all_re[:, L:], all_im[:, L:])
    lvl = jnp.concatenate([pw8[:, 0:2], pw8[:, 3:4]], axis=1)
    pw8 = pw8.transpose(0, 2, 1, 3)
    return toep, bsel, csel, lvl, pw8


def _s5_kernel(u_ref, toep_ref, bsel_ref, csel_ref, lvl_ref, pw8_ref, d_ref, y_ref,
               hre_ref, him_ref):
    bsz, chunks, width = u_ref.shape[1], u_ref.shape[2], u_ref.shape[3]
    lanes = lvl_ref.shape[-1]
    groups8 = chunks // 8
    row = lax.broadcasted_iota(jnp.int32, (chunks, lanes), 0)
    sub = lax.broadcasted_iota(jnp.int32, (groups8, 8, lanes), 1)
    toep = toep_ref[0]
    pw_re, pw_im = pw8_ref[0, 0], pw8_ref[0, 1]
    for pair in range(bsz // 2):
        us = [u_ref[0, 2 * pair + s] for s in range(2)]
        x = _dot(jnp.concatenate(us, axis=1), bsel_ref[0])
        xr = jnp.where(row >= 1, pltpu.roll(x[:, :lanes], 1, 0), 0.0).reshape(groups8, 8, lanes)
        xi = jnp.where(row >= 1, pltpu.roll(x[:, lanes:], 1, 0), 0.0).reshape(groups8, 8, lanes)
        for k, d in enumerate((1, 2, 4)):
            cr, ci = lvl_ref[0, k, 0:1, :], lvl_ref[0, k, 1:2, :]
            pr = jnp.where(sub >= d, pltpu.roll(xr, d, 1), 0.0)
            pi = jnp.where(sub >= d, pltpu.roll(xi, d, 1), 0.0)
            xr, xi = xr + (cr * pr - ci * pi), xi + (cr * pi + ci * pr)
        car = jnp.zeros((1, lanes), F32)
        cai = jnp.zeros((1, lanes), F32)
        for g in range(groups8):
            hr = xr[g] + (pw_re * car - pw_im * cai)
            hi = xi[g] + (pw_re * cai + pw_im * car)
            hre_ref[pair, 8 * g:8 * g + 8, :] = hr
            him_ref[pair, 8 * g:8 * g + 8, :] = hi
            car, cai = hr[7:8, :], hi[7:8, :]
        h = jnp.concatenate([hre_ref[pair], him_ref[pair]], axis=1).astype(BF16)
        y = _dot(h, csel_ref[0])
        for s in range(2):
            ys = y[:, s * width:(s + 1) * width] + _dot(us[s], toep)
            ys = ys + d_ref[0] * us[s].astype(F32)
            y_ref[0, 2 * pair + s] = ys.astype(y_ref.dtype)


def _s5(ug, toep, bsel, csel, lvl, pw8, d_rows):
    groups, bsz, chunks, width = ug.shape
    assert bsz % 2 == 0 and chunks % 8 == 0
    gspec = lambda a: pl.BlockSpec((1,) + a.shape[1:], lambda g: (g,) + (0,) * (a.ndim - 1))
    tile = pl.BlockSpec((1, bsz, chunks, width), lambda g: (g, 0, 0, 0))
    state = pltpu.VMEM((bsz // 2, chunks, lvl.shape[-1]), F32)
    return pl.pallas_call(
        _s5_kernel,
        grid=(groups,),
        in_specs=[tile, gspec(toep), gspec(bsel), gspec(csel), gspec(lvl), gspec(pw8),
                  gspec(d_rows)],
        out_specs=tile,
        out_shape=jax.ShapeDtypeStruct(ug.shape, BF16),
        scratch_shapes=[state, state],
        compiler_params=_cparams(("arbitrary",)),
        name="s5",
    )(ug, toep.astype(BF16), bsel.astype(BF16), csel.astype(BF16), lvl, pw8, d_rows)


def _gelu_tanh(x):
    return 0.5 * x * (1.0 + jnp.tanh(math.sqrt(2.0 / math.pi) * (x + 0.044715 * (x * x * x))))


def _out_odd_kernel(x_ref, yg_ref, gate_ref, mod_ref, gw_ref, gb_ref, unperm_ref, w_ref,
                    o_ref, ys_ref):
    d = x_ref.shape[-1]
    _s5_rows_to_bands(yg_ref, ys_ref)
    y = _gelu_tanh(ys_ref[...])
    y = y * jax.nn.sigmoid(_dot(y.astype(BF16), gw_ref[...]) + gb_ref[...])
    y = (y * _silu(gate_ref[0].astype(F32))).astype(BF16)
    y = _dot(unperm_ref[...], y).astype(BF16)
    o_ref[0] = x_ref[0] + mod_ref[0][:, 2 * d:] * _dot(y, w_ref[...])


def _out_odd(x, yg, gate, mod, glu_w, glu_b, w_out, tm):
    bsz, slen, d = x.shape
    groups = yg.shape[0]
    w = gate.shape[-1]
    rt = tm // S5_CHUNK
    tok = lambda wd: pl.BlockSpec((1, tm, wd), lambda b, i: (b, i, 0))
    return pl.pallas_call(
        _out_odd_kernel,
        grid=(bsz, slen // tm),
        in_specs=[tok(d), pl.BlockSpec((groups, 1, rt, S5_ROW), lambda b, i: (0, b, i, 0)), tok(w),
                  pl.BlockSpec((1, 1, 3 * d), lambda b, i: (b, 0, 0)),
                  _resident(glu_w.shape), _resident((1, w)), _resident((tm, tm)),
                  _resident(w_out.shape)],
        out_specs=tok(d),
        out_shape=jax.ShapeDtypeStruct((bsz, slen, d), F32),
        scratch_shapes=[pltpu.VMEM((tm, w), F32)],
        compiler_params=_cparams(("arbitrary", "arbitrary")),
        name="out_odd",
    )(x, yg, gate, mod, glu_w.astype(BF16), glu_b.reshape(1, w), _chunk_major_perm(tm).T,
      w_out.astype(BF16))


def _layer_pair(x, mod0, mod1, norm_g0, norm_g1, w_in_even, conv_w, conv_b, wr, br, wi, bi, lam,
                q_g, k_g, w_out_even, w_in_odd, lam_re, lam_im, log_dt, b_re, b_im, c_re, c_im,
                d_skip, glu_w, glu_b, w_out_odd):
    tm = 512
    xa, ga, q, k, v, gb = _in_even(x, mod0, norm_g0, w_in_even, q_g, k_g, tm=512)
    ya = _lru(xa, ga, conv_w, conv_b, wr, br, wi, bi, lam, ts=512)
    yb = _sb_attn(q, k, v, gb, tq=256, heads_per_step=8)
    x, ug, gate = _bridge(x, ya, yb, mod0, w_out_even, mod1, norm_g1, w_in_odd, tm=tm)
    groups = ug.shape[0]
    weights = _s5_weights(lam_re, lam_im, log_dt, b_re, b_im, c_re, c_im)
    d_rows = jnp.tile(d_skip.astype(F32).reshape(groups, 1, S5_GROUP), (1, 1, S5_CHUNK))
    yg = _s5(ug, *weights, d_rows)
    return _out_odd(x, yg, gate, mod1, glu_w, glu_b, w_out_odd, tm=tm)


def kernel(x, c, norm_g, ada_w, ada_b, w_in_even, conv_w, conv_b, lru_wr, lru_br, lru_wi, lru_bi,
           lru_lambda, q_norm_g, k_norm_g, w_out_even, w_in_odd, s5_lambda_re, s5_lambda_im,
           s5_log_dt, s5_b_re, s5_b_im, s5_c_re, s5_c_im, s5_d, glu_w, glu_b, w_out_odd):
    depth = norm_g.shape[0]
    assert depth % 2 == 0, "layers are fused in (even, odd) pairs"
    mods = _ada(c, ada_w, ada_b)[:, :, None, :]
    for j in range(depth // 2):
        x = _layer_pair(x, mods[2 * j], mods[2 * j + 1], norm_g[2 * j], norm_g[2 * j + 1],
                        w_in_even[j], conv_w[j], conv_b[j], lru_wr[j], lru_br[j], lru_wi[j],
                        lru_bi[j], lru_lambda[j], q_norm_g[j], k_norm_g[j], w_out_even[j],
                        w_in_odd[j], s5_lambda_re[j], s5_lambda_im[j], s5_log_dt[j], s5_b_re[j],
                        s5_b_im[j], s5_c_re[j], s5_c_im[j], s5_d[j], glu_w[j], glu_b[j],
                        w_out_odd[j])
    return x
```
